```python
import math
import jax, jax.numpy as jnp
from jax import lax
import numpy as np

D_MODEL = 2048
BATCH = 8
SEQ = 4096
DEPTH = 4

N_MIXERS = 3
N_SC = (DEPTH + 2) // 3
N_HY = (DEPTH + 1) // 3
N_MLA = DEPTH // 3
EPS = 1e-6

CONV_WIDTH = 3

HY_EMB = 33
HY_BANDS = (HY_EMB - 1) // 2
HY_FH = 64
HY_N_HID = 2
HY_FAST_DECAY = 0.3
HY_SLOW_DECAY = 1.5
HY_TARGET = 1e-2
HY_OUT_SCALE = 0.1

MLA_HEADS = D_MODEL // 128
Q_RANK = D_MODEL // 4
KV_RANK = D_MODEL // 4
D_NOPE = 128
D_ROPE = 64
D_V = 128
D_QK = D_NOPE + D_ROPE
ROPE_THETA = 10000.0
Q_BLOCK = 128

N_GROUPS = 8
EXPERTS_PER_GROUP = 4
N_EXPERTS = N_GROUPS * EXPERTS_PER_GROUP
TOP_K = 2
D_FF_EXPERT = D_MODEL // 4
MOE_BLOCK = 128

kernel_name = "hybrid_conv_hyena_mla_hiermoe_encoder"


def rms_norm(x, g):
    x32 = x.astype(jnp.float32)
    y = x32 * lax.rsqrt(jnp.mean(x32 * x32, axis=-1, keepdims=True) + EPS)
    return (y * g.astype(jnp.float32)).astype(x.dtype)


def centred_conv3(x, w):
    xp = jnp.pad(x, ((0, 0), (1, 1), (0, 0)))
    return xp[:, :-2] * w[0] + xp[:, 1:-1] * w[1] + xp[:, 2:] * w[2]


def short_conv_mixer(x, w_in, conv_w, w_out):
    bcx = x @ w_in
    b_gate, c_gate, xv = jnp.split(bcx, 3, axis=-1)
    y = b_gate * centred_conv3(c_gate * xv, conv_w)
    return y @ w_out


def hyena_filters(L, f_w1, f_b1, f_w_hid, f_b_hid, f_freq, f_w_out):
    f32 = jnp.float32
    t = jnp.linspace(0.0, 1.0, L, dtype=f32)[:, None]
    bands = jnp.linspace(1e-4, HY_BANDS - 1, HY_BANDS, dtype=f32)[None]
    w = 2.0 * math.pi * jnp.arange(L, dtype=f32)[:, None] / L
    z = jnp.concatenate([t, jnp.cos(bands * w), -jnp.sin(bands * w)], axis=-1)
    freq = f_freq.astype(f32)
    h = jnp.sin(freq[0] * (z @ f_w1.astype(f32) + f_b1.astype(f32)))
    for l in range(HY_N_HID):
        h = jnp.sin(freq[l + 1] * (h @ f_w_hid[l].astype(f32) + f_b_hid[l].astype(f32)))
    h = h @ f_w_out.astype(f32)
    d = h.shape[-1] // 2
    max_decay = math.log(HY_TARGET) / HY_FAST_DECAY
    min_decay = math.log(HY_TARGET) / HY_SLOW_DECAY
    deltas = jnp.abs(jnp.linspace(min_decay, max_decay, d, dtype=f32))
    decay = jnp.exp(-t * deltas[None])
    h = h.reshape(L, 2, d) * decay[:, None, :]
    return h[:, 0], h[:, 1]


def hyena_mixer(x, w_in, conv_w, conv_b, f_w1, f_b1, f_w_hid, f_b_hid, f_freq, f_w_out, skip, w_out):
    L = x.shape[1]
    u = centred_conv3(x @ w_in, conv_w) + conv_b
    x0, x1, v = jnp.split(u, 3, axis=-1)
    v = v * x1
    h_fwd, h_bwd = hyena_filters(L, f_w1, f_b1, f_w_hid, f_b_hid, f_freq, f_w_out)
    kern = jnp.concatenate([h_fwd, jnp.zeros_like(h_fwd[:1]), h_bwd[:0:-1]], axis=0)
    v32 = v.astype(jnp.float32)
    y = jnp.fft.irfft(jnp.fft.rfft(v32, n=2 * L, axis=1) * jnp.fft.rfft(kern, axis=0)[None],
                      n=2 * L, axis=1)[:, :L]
    y = (y + v32 * skip.astype(jnp.float32)).astype(x.dtype)
    return (y * x0) @ w_out


def rope_tables(L):
    inv = ROPE_THETA ** (-jnp.arange(0, D_ROPE, 2, dtype=jnp.float32) / D_ROPE)
    ang = jnp.arange(L, dtype=jnp.float32)[:, None] * inv[None]
    return jnp.cos(ang), jnp.sin(ang)


def apply_rope(x, cos, sin):
    x1, x2 = jnp.split(x, 2, axis=-1)
    c, s = cos[None, :, None, :], sin[None, :, None, :]
    return jnp.concatenate([x1 * c - x2 * s, x2 * c + x1 * s], axis=-1).astype(x.dtype)


def dense_attention(q, k, v):
    B_, L, H, dk = q.shape
    scale = dk ** -0.5
    qb = q.reshape(B_, L // Q_BLOCK, Q_BLOCK, H, dk).transpose(1, 0, 2, 3, 4)

    def one_block(q_blk):
        s = jnp.einsum('bqhd,bkhd->bhqk', q_blk, k).astype(jnp.float32) * scale
        p = jax.nn.softmax(s, axis=-1).astype(v.dtype)
        return jnp.einsum('bhqk,bkhd->bqhd', p, v)

    o = lax.map(one_block, qb)
    return o.transpose(1, 0, 2, 3, 4).reshape(B_, L, H, v.shape[-1])


def mla_mixer(x, w_down, q_norm, kv_norm, w_uq, w_ukv, q_gain, k_gain, w_o):
    B_, L, _ = x.shape
    down = x @ w_down
    c_q = rms_norm(down[..., :Q_RANK], q_norm)
    c_kv = rms_norm(down[..., Q_RANK:Q_RANK + KV_RANK], kv_norm)
    k_rope = down[..., Q_RANK + KV_RANK:]
    q = (c_q @ w_uq).reshape(B_, L, MLA_HEADS, D_QK)
    kv = (c_kv @ w_ukv).reshape(B_, L, MLA_HEADS, D_NOPE + D_V)
    k_nope, v = kv[..., :D_NOPE], kv[..., D_NOPE:]
    k = jnp.concatenate([k_nope, jnp.broadcast_to(k_rope[:, :, None, :], (B_, L, MLA_HEADS, D_ROPE))], axis=-1)
    q = rms_norm(q, q_gain)
    k = rms_norm(k, k_gain)
    cos, sin = rope_tables(L)
    q = jnp.concatenate([q[..., :D_NOPE], apply_rope(q[..., D_NOPE:], cos, sin)], axis=-1)
    k = jnp.concatenate([k[..., :D_NOPE], apply_rope(k[..., D_NOPE:], cos, sin)], axis=-1)
    o = dense_attention(q, k, v)
    return o.reshape(B_, L, MLA_HEADS * D_V) @ w_o


def hier_moe(x, w_group, b_group, w_expert, b_expert, w_up, w_down):
    B_, L, D_ = x.shape
    xf = x.reshape(-1, D_)
    N = xf.shape[0]
    f32 = jnp.float32
    x32 = xf.astype(f32)
    tok = jnp.arange(N)
    g_logits = x32 @ w_group.astype(f32) + b_group.astype(f32)
    g_prob = jax.nn.softmax(g_logits, axis=-1)
    g_sel = jnp.argmax(g_logits, axis=-1).astype(jnp.int32)
    e_logits = (x32 @ w_expert.astype(f32) + b_expert.astype(f32)).reshape(N, N_GROUPS, EXPERTS_PER_GROUP)
    e_in = e_logits[tok, g_sel]
    top_val, top_idx = lax.top_k(e_in, TOP_K)
    gate = g_prob[tok, g_sel][:, None] * jax.nn.softmax(top_val, axis=-1)
    expert_id = g_sel[:, None] * EXPERTS_PER_GROUP + top_idx.astype(jnp.int32)

    A = N * TOP_K
    flat_e = expert_id.reshape(A)
    flat_tok = jnp.repeat(jnp.arange(N, dtype=jnp.int32), TOP_K)
    flat_w = gate.reshape(A)
    order = jnp.argsort(flat_e)
    sorted_e = flat_e[order]
    counts = jnp.bincount(flat_e, length=N_EXPERTS)
    starts = jnp.cumsum(counts) - counts
    padded = (counts + MOE_BLOCK - 1) // MOE_BLOCK * MOE_BLOCK
    pend = jnp.cumsum(padded)
    pstart = pend - padded
    dest = pstart[sorted_e] + (jnp.arange(A) - starts[sorted_e])
    n_rows = A + N_EXPERTS * MOE_BLOCK
    n_blocks = n_rows // MOE_BLOCK
    row_tok = jnp.zeros((n_rows,), jnp.int32).at[dest].set(flat_tok[order])
    row_w = jnp.zeros((n_rows,), f32).at[dest].set(flat_w[order])
    block_e = jnp.minimum(jnp.searchsorted(pend, jnp.arange(n_blocks) * MOE_BLOCK, side='right'),
                          N_EXPERTS - 1)

    def expert_block(args):
        tok_b, e = args
        xb = xf[tok_b]
        hg, hu = jnp.split(xb @ w_up[e], 2, axis=-1)
        return (jax.nn.silu(hg) * hu) @ w_down[e]

    y_rows = lax.map(expert_block, (row_tok.reshape(n_blocks, MOE_BLOCK), block_e)).reshape(n_rows, D_)
    y_rows = y_rows * row_w[:, None].astype(y_rows.dtype)
    out = jax.ops.segment_sum(y_rows, row_tok, num_segments=N)
    return out.reshape(B_, L, D_)


def setup_inputs(seed: int = 0) -> dict:
    key = jax.random.key(seed)
    ks = iter(jax.random.split(key, 64))
    D = D_MODEL

    def nrm(shape, scale):
        return jax.random.normal(next(ks), shape, jnp.float32) * scale

    def gain(shape):
        return 1.0 + nrm(shape, 0.01)

    return {
        "x": nrm((BATCH, SEQ, D), 1.0),
        "norm_mix": gain((DEPTH, D)),
        "norm_ffn": gain((DEPTH, D)),
        "sc_w_in": nrm((N_SC, D, 3 * D), D ** -0.5),
        "sc_conv_w": nrm((N_SC, CONV_WIDTH, D), CONV_WIDTH ** -0.5),
        "sc_w_out": nrm((N_SC, D, D), D ** -0.5),
        "hy_w_in": nrm((N_HY, D, 3 * D), D ** -0.5),
        "hy_conv_w": nrm((N_HY, CONV_WIDTH, 3 * D), CONV_WIDTH ** -0.5),
        "hy_conv_b": nrm((N_HY, 3 * D), 0.01),
        "hy_filt_w1": nrm((N_HY, HY_EMB, HY_FH), HY_EMB ** -0.5),
        "hy_filt_b1": nrm((N_HY, HY_FH), 0.1),
        "hy_filt_w_hid": nrm((N_HY, HY_N_HID, HY_FH, HY_FH), HY_FH ** -0.5),
        "hy_filt_b_hid": nrm((N_HY, HY_N_HID, HY_FH), 0.1),
        "hy_filt_freq": gain((N_HY, HY_N_HID + 1, HY_FH)),
        "hy_filt_w_out": nrm((N_HY, HY_FH, 2 * D), HY_OUT_SCALE * HY_FH ** -0.5),
        "hy_skip": nrm((N_HY, D), 1.0),
        "hy_w_out": nrm((N_HY, D, D), D ** -0.5),
        "mla_w_down": nrm((N_MLA, D, Q_RANK + KV_RANK + D_ROPE), D ** -0.5),
        "mla_q_norm": gain((N_MLA, Q_RANK)),
        "mla_kv_norm": gain((N_MLA, KV_RANK)),
        "mla_w_uq": nrm((N_MLA, Q_RANK, MLA_HEADS * D_QK), Q_RANK ** -0.5),
        "mla_w_ukv": nrm((N_MLA, KV_RANK, MLA_HEADS * (D_NOPE + D_V)), KV_RANK ** -0.5),
        "mla_q_gain": gain((N_MLA, D_QK)),
        "mla_k_gain": gain((N_MLA, D_QK)),
        "mla_w_o": nrm((N_MLA, MLA_HEADS * D_V, D), (MLA_HEADS * D_V) ** -0.5),
        "router_w_group": nrm((DEPTH, D, N_GROUPS), D ** -0.5),
        "router_b_group": nrm((DEPTH, N_GROUPS), 0.01),
        "router_w_expert": nrm((DEPTH, D, N_EXPERTS), D ** -0.5),
        "router_b_expert": nrm((DEPTH, N_EXPERTS), 0.01),
        "moe_w_up": nrm((DEPTH, N_EXPERTS, D, 2 * D_FF_EXPERT), D ** -0.5),
        "moe_w_down": nrm((DEPTH, N_EXPERTS, D_FF_EXPERT, D), D_FF_EXPERT ** -0.5),
    }


def reference(x, norm_mix, norm_ffn,
              sc_w_in, sc_conv_w, sc_w_out,
              hy_w_in, hy_conv_w, hy_conv_b, hy_filt_w1, hy_filt_b1, hy_filt_w_hid, hy_filt_b_hid,
              hy_filt_freq, hy_filt_w_out, hy_skip, hy_w_out,
              mla_w_down, mla_q_norm, mla_kv_norm, mla_w_uq, mla_w_ukv, mla_q_gain, mla_k_gain, mla_w_o,
              router_w_group, router_b_group, router_w_expert, router_b_expert, moe_w_up, moe_w_down):
    for i in range(DEPTH):
        m = i % N_MIXERS
        j = i // N_MIXERS
        h = rms_norm(x, norm_mix[i])
        if m == 0:
            y = short_conv_mixer(h, sc_w_in[j], sc_conv_w[j], sc_w_out[j])
        elif m == 1:
            y = hyena_mixer(h, hy_w_in[j], hy_conv_w[j], hy_conv_b[j], hy_filt_w1[j], hy_filt_b1[j],
                            hy_filt_w_hid[j], hy_filt_b_hid[j], hy_filt_freq[j], hy_filt_w_out[j],
                            hy_skip[j], hy_w_out[j])
        else:
            y = mla_mixer(h, mla_w_down[j], mla_q_norm[j], mla_kv_norm[j], mla_w_uq[j], mla_w_ukv[j],
                          mla_q_gain[j], mla_k_gain[j], mla_w_o[j])
        x = x + y
        h = rms_norm(x, norm_ffn[i])
        x = x + hier_moe(h, router_w_group[i], router_b_group[i], router_w_expert[i], router_b_expert[i],
                         moe_w_up[i], moe_w_down[i])
    return x
```

```python
import functools
import math

import jax
import jax.numpy as jnp
from jax import lax
from jax.experimental import pallas as pl
from jax.experimental.pallas import tpu as pltpu

F32 = jnp.float32
BF16 = jnp.bfloat16
I32 = jnp.int32
EPS = 1e-6

LANE = 128
HALO = 16
VMEM_LIMIT_BYTES = 56 * 1024 * 1024

D_NOPE = 128
D_ROPE = 64
D_V = 128
D_QK = D_NOPE + D_ROPE
D_QK_PAD = 256
ROPE_THETA = 10000.0
TOP_K = 2
HY_FAST_DECAY = 0.3
HY_SLOW_DECAY = 1.5
HY_TARGET = 1e-2

HIGHEST = lax.Precision.HIGHEST


def _tile(dim, want):
    t = min(dim, want)
    assert dim % t == 0, (dim, want)
    return t


def _call(kernel, *, grid, in_specs, out_specs, out_shape, semantics, name, scratch=(), num_prefetch=0):
    grid_spec = pltpu.PrefetchScalarGridSpec(
        num_scalar_prefetch=num_prefetch, grid=grid, in_specs=in_specs, out_specs=out_specs,
        scratch_shapes=list(scratch))
    return pl.pallas_call(
        kernel, grid_spec=grid_spec, out_shape=out_shape, name=name,
        compiler_params=pltpu.CompilerParams(dimension_semantics=semantics, vmem_limit_bytes=VMEM_LIMIT_BYTES))


def _dot(a, b, precision=None):
    return jnp.dot(a, b, preferred_element_type=F32, precision=precision)


def _rms(x, g):
    return x * lax.rsqrt(jnp.mean(x * x, axis=-1, keepdims=True) + EPS) * g


def _conv3_rows(z, cw, lo, hi, tm):
    rows = z.shape[0]
    r = lax.broadcasted_iota(I32, z.shape, 0)
    z = jnp.where((r < lo) | (r >= hi), 0.0, z)
    zm = pltpu.roll(z, 1, 0)
    zp = pltpu.roll(z, rows - 1, 0)
    y = zm * cw[0:1] + z * cw[1:2] + zp * cw[2:3]
    return y[HALO:HALO + tm]


def _inproj_kernel(x_ref, xp_ref, xn_ref, g_ref, wa_ref, wb_ref, wc_ref, cw_ref, cb_ref, *rest,
                   hyena, seq_tiles, tm):
    hs_ref = rest[-1]
    i = pl.program_id(0)

    @pl.when(pl.program_id(1) == 0)
    def _():
        g = g_ref[...]
        hs_ref[0:HALO, :] = _rms(xp_ref[...], g).astype(BF16)
        hs_ref[HALO:HALO + tm, :] = _rms(x_ref[...], g).astype(BF16)
        hs_ref[HALO + tm:, :] = _rms(xn_ref[...], g).astype(BF16)

    it = i % seq_tiles
    lo = jnp.where(it == 0, HALO, 0)
    hi = jnp.where(it == seq_tiles - 1, HALO + tm, tm + 2 * HALO)
    hs = hs_ref[...]
    za = _dot(hs, wa_ref[...])
    zb = _dot(hs, wb_ref[...])
    zc = _dot(hs, wc_ref[...])
    if hyena:
        x0_ref, vv_ref = rest[0], rest[1]
        x0 = _conv3_rows(za, cw_ref[0], lo, hi, tm) + cb_ref[0]
        x1 = _conv3_rows(zb, cw_ref[1], lo, hi, tm) + cb_ref[1]
        v = _conv3_rows(zc, cw_ref[2], lo, hi, tm) + cb_ref[2]
        x0_ref[...] = x0.astype(BF16)
        vv_ref[...] = (v * x1).astype(BF16)
    else:
        y_ref = rest[0]
        y = za[HALO:HALO + tm] * _conv3_rows(zb * zc, cw_ref[0], lo, hi, tm)
        y_ref[...] = y.astype(BF16)


def _inproj(x, g, w_in, cw, cb, *, hyena, seq):
    n, d = x.shape
    tm = _tile(seq, 512)
    tn = _tile(d, 512)
    nj = d // tn
    hb = tm // HALO
    last_hb = n // HALO - 1
    groups = cw.shape[0]
    kern = functools.partial(_inproj_kernel, hyena=hyena, seq_tiles=seq // tm, tm=tm)
    n_out = 2 if hyena else 1
    out_spec = pl.BlockSpec((tm, tn), lambda i, j: (i, j))
    outs = _call(
        kern, grid=(n // tm, nj),
        in_specs=[
            pl.BlockSpec((tm, d), lambda i, j: (i, 0)),
            pl.BlockSpec((HALO, d), lambda i, j: (jnp.maximum(i * hb - 1, 0), 0)),
            pl.BlockSpec((HALO, d), lambda i, j: (jnp.minimum((i + 1) * hb, last_hb), 0)),
            pl.BlockSpec((1, d), lambda i, j: (0, 0)),
            pl.BlockSpec((d, tn), lambda i, j: (0, j)),
            pl.BlockSpec((d, tn), lambda i, j: (0, nj + j)),
            pl.BlockSpec((d, tn), lambda i, j: (0, 2 * nj + j)),
            pl.BlockSpec((groups, 3, tn), lambda i, j: (0, 0, j)),
            pl.BlockSpec((groups, 1, tn), lambda i, j: (0, 0, j)),
        ],
        out_specs=[out_spec] * n_out,
        out_shape=[jax.ShapeDtypeStruct((n, d), BF16)] * n_out,
        scratch=[pltpu.VMEM((tm + 2 * HALO, d), BF16)],
        semantics=("parallel", "arbitrary"),
        name="hyena_inproj" if hyena else "shortconv_inproj",
    )(x, x, x, g, w_in, w_in, w_in, cw, cb)
    return outs


def _matmul_res_kernel(a_ref, w_ref, r_ref, o_ref):
    o_ref[...] = r_ref[...] + _dot(a_ref[...], w_ref[...])


def _matmul_res(a, w, res, name):
    n, k = a.shape
    dout = w.shape[1]
    tm = _tile(n, 1024)
    tn = _tile(dout, 1024)
    return _call(
        _matmul_res_kernel, grid=(n // tm, dout // tn),
        in_specs=[
            pl.BlockSpec((tm, k), lambda i, j: (i, 0)),
            pl.BlockSpec((k, tn), lambda i, j: (0, j)),
            pl.BlockSpec((tm, tn), lambda i, j: (i, j)),
        ],
        out_specs=pl.BlockSpec((tm, tn), lambda i, j: (i, j)),
        out_shape=jax.ShapeDtypeStruct((n, dout), F32),
        semantics=("parallel", "parallel"),
        name=name,
    )(a, w, res)


def _filter_kernel(z_ref, w1_ref, b1_ref, wh_ref, bh_ref, fr_ref, wo_ref, dl_ref, o_ref, h_ref, *, n_hid, n_half):
    j = pl.program_id(0)

    @pl.when(j == 0)
    def _():
        h = jnp.sin(fr_ref[0:1, :] * (_dot(z_ref[...], w1_ref[...], HIGHEST) + b1_ref[...]))
        for l in range(n_hid):
            h = jnp.sin(fr_ref[l + 1:l + 2, :] * (_dot(h, wh_ref[l], HIGHEST) + bh_ref[l:l + 1, :]))
        h_ref[...] = h

    f = _dot(h_ref[...], wo_ref[...], HIGHEST)
    t = z_ref[:, 0:1]
    f = f * jnp.exp(-t * dl_ref[...])
    row = lax.broadcasted_iota(I32, f.shape, 0)
    f = jnp.where(row < jnp.where(j >= n_half, 1, 0), 0.0, f)
    o_ref[...] = f.astype(BF16)


def _pad_to(a, shape):
    return jnp.pad(a, [(0, s - d) for s, d in zip(shape, a.shape)])


def _hyena_filters(seq, d, f_w1, f_b1, f_w_hid, f_b_hid, f_freq, f_w_out):
    emb, fh = f_w1.shape
    n_hid = f_w_hid.shape[0]
    bands_n = (emb - 1) // 2
    t = jnp.linspace(0.0, 1.0, seq, dtype=F32)[:, None]
    bands = jnp.linspace(1e-4, bands_n - 1, bands_n, dtype=F32)[None]
    w = 2.0 * math.pi * jnp.arange(seq, dtype=F32)[:, None] / seq
    z = jnp.concatenate([t, jnp.cos(bands * w), -jnp.sin(bands * w)], axis=-1)
    max_decay = math.log(HY_TARGET) / HY_FAST_DECAY
    min_decay = math.log(HY_TARGET) / HY_SLOW_DECAY
    deltas = jnp.abs(jnp.linspace(min_decay, max_decay, d, dtype=F32))[None]
    deltas2 = jnp.concatenate([deltas, deltas], axis=-1)
    tn = _tile(d, 512)
    kern = functools.partial(_filter_kernel, n_hid=n_hid, n_half=d // tn)
    full = lambda shape: pl.BlockSpec(shape, lambda j: (0,) * len(shape))
    return _call(
        kern, grid=(2 * d // tn,),
        in_specs=[
            full((seq, LANE)), full((LANE, LANE)), full((1, LANE)), full((n_hid, LANE, LANE)),
            full((n_hid, LANE)), full((n_hid + 1, LANE)),
            pl.BlockSpec((LANE, tn), lambda j: (0, j)),
            pl.BlockSpec((1, tn), lambda j: (0, j)),
        ],
        out_specs=pl.BlockSpec((seq, tn), lambda j: (0, j)),
        out_shape=jax.ShapeDtypeStruct((seq, 2 * d), BF16),
        scratch=[pltpu.VMEM((seq, LANE), F32)],
        semantics=("arbitrary",),
        name="hyena_filter",
    )(_pad_to(z, (seq, LANE)), _pad_to(f_w1, (LANE, LANE)), _pad_to(f_b1[None], (1, LANE)),
      _pad_to(f_w_hid, (n_hid, LANE, LANE)), _pad_to(f_b_hid, (n_hid, LANE)), _pad_to(f_freq, (n_hid + 1, LANE)),
      _pad_to(f_w_out, (LANE, 2 * d)), deltas2)


def _dft_matrices(seq):
    n = 2 * seq
    k = jnp.arange(seq, dtype=I32)
    ang = ((k[:, None] * k[None, :]) % n).astype(F32) * (2.0 * math.pi / n)
    c = jnp.cos(ang)
    s = jnp.sin(ang)
    alt = jnp.where(k % 2 == 0, 1.0, -1.0).astype(F32)
    wc = c
    ws = jnp.where(k[:, None] == 0, alt[None, :], -s)
    wk = jnp.where(k == 0, 1.0, 2.0).astype(F32) / n
    gc = c * wk[None, :]
    gs = jnp.where(k[None, :] == 0, alt[:, None] / n, -s * wk[None, :])
    return wc.astype(BF16), ws.astype(BF16), gc.astype(BF16), gs.astype(BF16)


def _dc_rows(shape, freq_tile):
    row = lax.broadcasted_iota(I32, shape, 0)
    return row < jnp.where(freq_tile == 0, 1, 0)


def _filter_spec_kernel(wc_ref, ws_ref, hf_ref, hb_ref, ha_ref, hbo_ref):
    wc = wc_ref[...]
    ws = ws_ref[...]
    hf = hf_ref[...]
    hb = hb_ref[...]
    ha_ref[...] = _dot(wc, hf) + _dot(wc, hb)
    bf = _dot(ws, hf)
    bb = _dot(ws, hb)
    hbo_ref[...] = jnp.where(_dc_rows(bf.shape, pl.program_id(0)), bf + bb, bf - bb)


def _signal_spec_kernel(wc_ref, ws_ref, v_ref, ha_ref, hb_ref, ya_ref, yb_ref):
    v = v_ref[...]
    a = _dot(wc_ref[...], v)
    b = _dot(ws_ref[...], v)
    ha = ha_ref[...]
    hb = hb_ref[...]
    dc = _dc_rows(a.shape, pl.program_id(0))
    ya_ref[...] = jnp.where(dc, a * ha, a * ha - b * hb).astype(BF16)
    yb_ref[...] = jnp.where(dc, b * hb, a * hb + b * ha).astype(BF16)


def _inverse_kernel(gc_ref, gs_ref, ya_ref, yb_ref, vv_ref, x0_ref, sk_ref, o_ref):
    y = _dot(gc_ref[...], ya_ref[...]) + _dot(gs_ref[...], yb_ref[...])
    y = y + vv_ref[...].astype(F32) * sk_ref[...]
    o_ref[...] = (y * x0_ref[...].astype(F32)).astype(BF16)


def _hyena_long_conv(vv, x0, filt, skip, batch, seq):
    d = vv.shape[1]
    wc, ws, gc, gs = _dft_matrices(seq)
    tk = _tile(seq, 512)
    tn = _tile(d, 512)
    nj = d // tn
    ha, hb = _call(
        _filter_spec_kernel, grid=(seq // tk, nj),
        in_specs=[
            pl.BlockSpec((tk, seq), lambda kt, j: (kt, 0)),
            pl.BlockSpec((tk, seq), lambda kt, j: (kt, 0)),
            pl.BlockSpec((seq, tn), lambda kt, j: (0, j)),
            pl.BlockSpec((seq, tn), lambda kt, j: (0, nj + j)),
        ],
        out_specs=[pl.BlockSpec((tk, tn), lambda kt, j: (kt, j))] * 2,
        out_shape=[jax.ShapeDtypeStruct((seq, d), F32)] * 2,
        semantics=("parallel", "parallel"),
        name="hyena_filter_spectrum",
    )(wc, ws, filt, filt)

    v3 = vv.reshape(batch, seq, d)
    ya, yb = _call(
        _signal_spec_kernel, grid=(seq // tk, batch, nj),
        in_specs=[
            pl.BlockSpec((tk, seq), lambda kt, b, j: (kt, 0)),
            pl.BlockSpec((tk, seq), lambda kt, b, j: (kt, 0)),
            pl.BlockSpec((None, seq, tn), lambda kt, b, j: (b, 0, j)),
            pl.BlockSpec((tk, tn), lambda kt, b, j: (kt, j)),
            pl.BlockSpec((tk, tn), lambda kt, b, j: (kt, j)),
        ],
        out_specs=[pl.BlockSpec((None, tk, tn), lambda kt, b, j: (b, kt, j))] * 2,
        out_shape=[jax.ShapeDtypeStruct((batch, seq, d), BF16)] * 2,
        semantics=("parallel", "parallel", "parallel"),
        name="hyena_signal_spectrum",
    )(wc, ws, v3, ha, hb)

    tt = _tile(seq, 512)
    out = _call(
        _inverse_kernel, grid=(seq // tt, batch, nj),
        in_specs=[
            pl.BlockSpec((tt, seq), lambda ti, b, j: (ti, 0)),
            pl.BlockSpec((tt, seq), lambda ti, b, j: (ti, 0)),
            pl.BlockSpec((None, seq, tn), lambda ti, b, j: (b, 0, j)),
            pl.BlockSpec((None, seq, tn), lambda ti, b, j: (b, 0, j)),
            pl.BlockSpec((None, tt, tn), lambda ti, b, j: (b, ti, j)),
            pl.BlockSpec((None, tt, tn), lambda ti, b, j: (b, ti, j)),
            pl.BlockSpec((1, tn), lambda ti, b, j: (0, j)),
        ],
        out_specs=pl.BlockSpec((None, tt, tn), lambda ti, b, j: (b, ti, j)),
        out_shape=jax.ShapeDtypeStruct((batch, seq, d), BF16),
        semantics=("parallel", "parallel", "parallel"),
        name="hyena_inverse",
    )(gc, gs, ya, yb, v3, x0.reshape(batch, seq, d), skip)
    return out.reshape(batch * seq, d)


def _mla_down_kernel(x_ref, g_ref, w_ref, qn_ref, kvn_ref, cq_ref, ckv_ref, kr_ref, *, q_rank, kv_rank):
    h = _rms(x_ref[...], g_ref[...]).astype(BF16)
    d = _dot(h, w_ref[...])
    cq_ref[...] = _rms(d[:, :q_rank], qn_ref[...]).astype(BF16)
    ckv_ref[...] = _rms(d[:, q_rank:q_rank + kv_rank], kvn_ref[...]).astype(BF16)
    kr_ref[...] = d[:, q_rank + kv_rank:]


def _rope_head(nope, rope, rope_swapped, gains, cos, sin, scale):
    ms = (jnp.sum(nope * nope, axis=-1, keepdims=True) + jnp.sum(rope * rope, axis=-1, keepdims=True)) / D_QK
    r = lax.rsqrt(ms + EPS) * scale
    out_nope = nope * r * gains[:, :LANE]
    out_rope = r * (rope * gains[:, LANE:2 * LANE] * cos + rope_swapped * gains[:, 2 * LANE:] * sin)
    return out_nope, out_rope


def _mla_q_kernel(cq_ref, w_ref, gn_ref, cos_ref, sin_ref, q_ref, *, heads, scale):
    z = _dot(cq_ref[...], w_ref[...])
    cos = cos_ref[...]
    sin = sin_ref[...]
    gains = gn_ref[...]
    for h in range(heads):
        zh = z[:, h * 3 * LANE:(h + 1) * 3 * LANE]
        qn, qr = _rope_head(zh[:, :LANE], zh[:, LANE:2 * LANE], zh[:, 2 * LANE:], gains, cos, sin, scale)
        q_ref[:, h * D_QK_PAD:h * D_QK_PAD + LANE] = qn.astype(BF16)
        q_ref[:, h * D_QK_PAD + LANE:(h + 1) * D_QK_PAD] = qr.astype(BF16)


def _mla_kv_kernel(ckv_ref, w_ref, kr_ref, gn_ref, cos_ref, sin_ref, k_ref, v_ref, *, heads):
    z = _dot(ckv_ref[...], w_ref[...])
    kr = kr_ref[...]
    cos = cos_ref[...]
    sin = sin_ref[...]
    gains = gn_ref[...]
    for h in range(heads):
        zh = z[:, h * 2 * LANE:(h + 1) * 2 * LANE]
        kn, krot = _rope_head(zh[:, :LANE], kr[:, :LANE], kr[:, LANE:], gains, cos, sin, 1.0)
        k_ref[:, h * D_QK_PAD:h * D_QK_PAD + LANE] = kn.astype(BF16)
        k_ref[:, h * D_QK_PAD + LANE:(h + 1) * D_QK_PAD] = krot.astype(BF16)
        v_ref[:, h * D_V:(h + 1) * D_V] = zh[:, LANE:].astype(BF16)


def _attention_kernel(q_ref, k_ref, v_ref, o_ref):
    s = lax.dot_general(q_ref[...], k_ref[...], (((1,), (1,)), ((), ())), preferred_element_type=F32)
    m = jnp.max(s, axis=-1, keepdims=True)
    p = jnp.exp(s - m)
    l = jnp.sum(p, axis=-1, keepdims=True)
    o = _dot(p.astype(BF16), v_ref[...])
    o_ref[...] = (o / l).astype(BF16)


def _swap_halves(a):
    half = a.shape[-1] // 2
    return jnp.concatenate([a[..., half:], a[..., :half]], axis=-1)


def _lane_pad(a):
    return jnp.pad(a, [(0, 0)] * (a.ndim - 1) + [(0, LANE - a.shape[-1])])


def _rope_gains(gain):
    rope = gain[D_NOPE:]
    return jnp.concatenate([gain[:D_NOPE], _lane_pad(rope), _lane_pad(_swap_halves(rope))])[None]


def _mla_mixer(x, g, w_down, q_norm, kv_norm, w_uq, w_ukv, q_gain, k_gain, w_o, batch, seq):
    n, d = x.shape
    q_rank = q_norm.shape[0]
    kv_rank = kv_norm.shape[0]
    heads = w_o.shape[0] // D_V

    w_kr = w_down[:, q_rank + kv_rank:]
    w_down_p = jnp.concatenate(
        [w_down[:, :q_rank + kv_rank], _lane_pad(w_kr), _lane_pad(_swap_halves(w_kr))], axis=-1).astype(BF16)
    wq = w_uq.reshape(q_rank, heads, D_QK)
    wq_rope = wq[..., D_NOPE:]
    wq_p = jnp.concatenate([wq[..., :D_NOPE], _lane_pad(wq_rope), _lane_pad(_swap_halves(wq_rope))], axis=-1)
    wq_p = wq_p.reshape(q_rank, heads * 3 * LANE).astype(BF16)
    wkv = w_ukv.astype(BF16)

    inv = ROPE_THETA ** (-jnp.arange(0, D_ROPE, 2, dtype=F32) / D_ROPE)
    ang = jnp.arange(seq, dtype=F32)[:, None] * inv[None]
    cos = _lane_pad(jnp.concatenate([jnp.cos(ang), jnp.cos(ang)], axis=-1))
    sin = _lane_pad(jnp.concatenate([-jnp.sin(ang), jnp.sin(ang)], axis=-1))

    tm = _tile(seq, 512)
    seq_tiles = seq // tm
    dw = w_down_p.shape[1]
    cq, ckv, kr = _call(
        functools.partial(_mla_down_kernel, q_rank=q_rank, kv_rank=kv_rank), grid=(n // tm,),
        in_specs=[
            pl.BlockSpec((tm, d), lambda i: (i, 0)),
            pl.BlockSpec((1, d), lambda i: (0, 0)),
            pl.BlockSpec((d, dw), lambda i: (0, 0)),
            pl.BlockSpec((1, q_rank), lambda i: (0, 0)),
            pl.BlockSpec((1, kv_rank), lambda i: (0, 0)),
        ],
        out_specs=[
            pl.BlockSpec((tm, q_rank), lambda i: (i, 0)),
            pl.BlockSpec((tm, kv_rank), lambda i: (i, 0)),
            pl.BlockSpec((tm, 2 * LANE), lambda i: (i, 0)),
        ],
        out_shape=[
            jax.ShapeDtypeStruct((n, q_rank), BF16),
            jax.ShapeDtypeStruct((n, kv_rank), BF16),
            jax.ShapeDtypeStruct((n, 2 * LANE), F32),
        ],
        semantics=("parallel",),
        name="mla_down",
    )(x, g, w_down_p, q_norm[None], kv_norm[None])

    hb = _tile(heads, 4)
    table_spec = pl.BlockSpec((tm, LANE), lambda i, j: (i % seq_tiles, 0))
    q = _call(
        functools.partial(_mla_q_kernel, heads=hb, scale=D_QK ** -0.5), grid=(n // tm, heads // hb),
        in_specs=[
            pl.BlockSpec((tm, q_rank), lambda i, j: (i, 0)),
            pl.BlockSpec((q_rank, hb * 3 * LANE), lambda i, j: (0, j)),
            pl.BlockSpec((1, 3 * LANE), lambda i, j: (0, 0)),
            table_spec, table_spec,
        ],
        out_specs=pl.BlockSpec((tm, hb * D_QK_PAD), lambda i, j: (i, j)),
        out_shape=jax.ShapeDtypeStruct((n, heads * D_QK_PAD), BF16),
        semantics=("parallel", "parallel"),
        name="mla_q",
    )(cq, wq_p, _rope_gains(q_gain), cos, sin)

    k, v = _call(
        functools.partial(_mla_kv_kernel, heads=hb), grid=(n // tm, heads // hb),
        in_specs=[
            pl.BlockSpec((tm, kv_rank), lambda i, j: (i, 0)),
            pl.BlockSpec((kv_rank, hb * 2 * LANE), lambda i, j: (0, j)),
            pl.BlockSpec((tm, 2 * LANE), lambda i, j: (i, 0)),
            pl.BlockSpec((1, 3 * LANE), lambda i, j: (0, 0)),
            table_spec, table_spec,
        ],
        out_specs=[
            pl.BlockSpec((tm, hb * D_QK_PAD), lambda i, j: (i, j)),
            pl.BlockSpec((tm, hb * D_V), lambda i, j: (i, j)),
        ],
        out_shape=[
            jax.ShapeDtypeStruct((n, heads * D_QK_PAD), BF16),
            jax.ShapeDtypeStruct((n, heads * D_V), BF16),
        ],
        semantics=("parallel", "parallel"),
        name="mla_kv",
    )(ckv, wkv, kr, _rope_gains(k_gain), cos, sin)

    tq = _tile(seq, 256)
    qt = seq // tq
    o = _call(
        _attention_kernel, grid=(batch, heads, qt),
        in_specs=[
            pl.BlockSpec((tq, D_QK_PAD), lambda b, h, i: (b * qt + i, h)),
            pl.BlockSpec((seq, D_QK_PAD), lambda b, h, i: (b, h)),
            pl.BlockSpec((seq, D_V), lambda b, h, i: (b, h)),
        ],
        out_specs=pl.BlockSpec((tq, D_V), lambda b, h, i: (b * qt + i, h)),
        out_shape=jax.ShapeDtypeStruct((n, heads * D_V), BF16),
        semantics=("parallel", "parallel", "parallel"),
        name="mla_attention",
    )(q, k, v)
    return _matmul_res(o, w_o.astype(BF16), x, "mla_out")


def _router_kernel(x_ref, g_ref, w_ref, b_ref, h_ref, ids_ref, gates_ref, *, n_groups, per_group):
    h = _rms(x_ref[...], g_ref[...])
    h_ref[...] = h
    logits = _dot(h, w_ref[...], HIGHEST) + b_ref[...]
    lane = lax.broadcasted_iota(I32, logits.shape, 1)
    neg = -jnp.inf

    def first_argmax(vals, vmax):
        return jnp.min(jnp.where(vals == vmax, lane, LANE), axis=-1, keepdims=True)

    gl = jnp.where(lane < n_groups, logits, neg)
    gmax = jnp.max(gl, axis=-1, keepdims=True)
    g_sel = first_argmax(gl, gmax)
    g_prob = 1.0 / jnp.sum(jnp.exp(gl - gmax), axis=-1, keepdims=True)
    lo = n_groups + g_sel * per_group
    el = jnp.where((lane >= lo) & (lane < lo + per_group), logits, neg)
    t1 = jnp.max(el, axis=-1, keepdims=True)
    i1 = first_argmax(el, t1)
    el2 = jnp.where(lane == i1, neg, el)
    t2 = jnp.max(el2, axis=-1, keepdims=True)
    i2 = first_argmax(el2, t2)
    d = jnp.exp(t2 - t1)
    p1 = 1.0 / (1.0 + d)
    ids_ref[...] = jnp.where(lane == 0, i1 - n_groups, jnp.where(lane == 1, i2 - n_groups, 0))
    gates_ref[...] = jnp.where(lane == 0, g_prob * p1, jnp.where(lane == 1, g_prob * (d * p1), 0.0))


def _expert_kernel(block_e_ref, row_tok_ref, h_hbm, wup_ref, wdn_ref, roww_ref, y_ref, xbuf, sem, *, tm, d_ff):
    i = pl.program_id(0)
    n_tiles = pl.num_programs(0)

    def gather_rows(tile, slot):
        base = tile * tm

        def body(r, carry):
            tok = row_tok_ref[base + r]
            pltpu.make_async_copy(h_hbm.at[pl.ds(tok, 1)], xbuf.at[slot, pl.ds(r, 1)], sem.at[slot]).start()
            return carry

        lax.fori_loop(0, tm, body, 0)

    @pl.when(i == 0)
    def _():
        gather_rows(0, 0)

    slot = i % 2

    @pl.when(i + 1 < n_tiles)
    def _():
        gather_rows(i + 1, 1 - slot)

    pltpu.make_async_copy(xbuf.at[slot], xbuf.at[slot], sem.at[slot]).wait()
    xb = xbuf[slot].astype(BF16)
    hgu = _dot(xb, wup_ref[...])
    hg = hgu[:, :d_ff]
    hu = hgu[:, d_ff:]
    act = (hg * jax.nn.sigmoid(hg) * hu).astype(BF16)
    y_ref[...] = _dot(act, wdn_ref[...]) * roww_ref[...]


def _combine_kernel(pos_ref, x_ref, y_hbm, o_ref, ybuf, sem, *, tt):
    i = pl.program_id(0)
    n_tiles = pl.num_programs(0)

    def gather_rows(tile, slot):
        base = tile * tt

        def body(r, carry):
            for k in range(TOP_K):
                row = pos_ref[TOP_K * (base + r) + k]
                pltpu.make_async_copy(y_hbm.at[pl.ds(row, 1)], ybuf.at[slot, k, pl.ds(r, 1)], sem.at[slot]).start()
            return carry

        lax.fori_loop(0, tt, body, 0)

    @pl.when(i == 0)
    def _():
        gather_rows(0, 0)

    slot = i % 2

    @pl.when(i + 1 < n_tiles)
    def _():
        gather_rows(i + 1, 1 - slot)

    pltpu.make_async_copy(ybuf.at[slot], ybuf.at[slot], sem.at[slot]).wait()
    acc = x_ref[...]
    for k in range(TOP_K):
        acc = acc + ybuf[slot, k]
    o_ref[...] = acc


def _moe_plan(ids, gates, n_experts, tm):
    n = ids.shape[0]
    a = n * TOP_K
    flat_e = ids.reshape(a)
    flat_w = gates.reshape(a)
    onehot = (flat_e[:, None] == jnp.arange(n_experts, dtype=I32)[None]).astype(I32)
    csum = jnp.cumsum(onehot, axis=0)
    counts = csum[-1]
    rank = jnp.sum(csum * onehot, axis=1) - 1
    starts = jnp.cumsum(counts) - counts
    padded = (counts + tm - 1) // tm * tm
    pend = jnp.cumsum(padded)
    pstart = pend - padded
    pos = pstart[flat_e] + rank
    n_rows = a + n_experts * tm
    n_tiles = n_rows // tm
    block_e = jnp.minimum(jnp.searchsorted(pend, jnp.arange(n_tiles, dtype=I32) * tm, side='right'),
                          n_experts - 1).astype(I32)
    order = jnp.argsort(flat_e, stable=True).astype(I32)
    e_r = jnp.repeat(block_e, tm)
    off = jnp.arange(n_rows, dtype=I32) - pstart[e_r]
    valid = off < counts[e_r]
    src = order[jnp.clip(starts[e_r] + off, 0, a - 1)]
    row_tok = jnp.where(valid, src // TOP_K, 0).astype(I32)
    row_w = jnp.where(valid, flat_w[src], 0.0)
    return block_e, row_tok, row_w[:, None], pos.astype(I32)


def _hier_moe(x, g, w_group, b_group, w_expert, b_expert, w_up, w_down):
    n, d = x.shape
    n_groups = w_group.shape[1]
    n_experts = w_expert.shape[1]
    d_ff = w_down.shape[1]
    w_r = _pad_to(jnp.concatenate([w_group, w_expert], axis=1), (d, LANE))
    b_r = _pad_to(jnp.concatenate([b_group, b_expert])[None], (1, LANE))
    tr = _tile(n, 512)
    h, ids, gates = _call(
        functools.partial(_router_kernel, n_groups=n_groups, per_group=n_experts // n_groups), grid=(n // tr,),
        in_specs=[
            pl.BlockSpec((tr, d), lambda i: (i, 0)),
            pl.BlockSpec((1, d), lambda i: (0, 0)),
            pl.BlockSpec((d, LANE), lambda i: (0, 0)),
            pl.BlockSpec((1, LANE), lambda i: (0, 0)),
        ],
        out_specs=[
            pl.BlockSpec((tr, d), lambda i: (i, 0)),
            pl.BlockSpec((tr, LANE), lambda i: (i, 0)),
            pl.BlockSpec((tr, LANE), lambda i: (i, 0)),
        ],
        out_shape=[
            jax.ShapeDtypeStruct((n, d), F32),
            jax.ShapeDtypeStruct((n, LANE), I32),
            jax.ShapeDtypeStruct((n, LANE), F32),
        ],
        semantics=("parallel",),
        name="moe_router",
    )(x, g, w_r, b_r)

    tm = 256 if n * TOP_K >= 256 * n_experts else 128
    block_e, row_tok, row_w, pos = _moe_plan(ids[:, :TOP_K], gates[:, :TOP_K], n_experts, tm)
    n_rows = row_tok.shape[0]
    y_rows = _call(
        functools.partial(_expert_kernel, tm=tm, d_ff=d_ff), grid=(n_rows // tm,), num_prefetch=2,
        in_specs=[
            pl.BlockSpec(memory_space=pl.ANY),
            pl.BlockSpec((None, d, 2 * d_ff), lambda i, be, rt: (be[i], 0, 0)),
            pl.BlockSpec((None, d_ff, d), lambda i, be, rt: (be[i], 0, 0)),
            pl.BlockSpec((tm, 1), lambda i, be, rt: (i, 0)),
        ],
        out_specs=pl.BlockSpec((tm, d), lambda i, be, rt: (i, 0)),
        out_shape=jax.ShapeDtypeStruct((n_rows, d), F32),
        scratch=[pltpu.VMEM((2, tm, d), F32), pltpu.SemaphoreType.DMA((2,))],
        semantics=("arbitrary",),
        name="moe_experts",
    )(block_e, row_tok, h, w_up.astype(BF16), w_down.astype(BF16), row_w)

    tt = _tile(n, 128)
    return _call(
        functools.partial(_combine_kernel, tt=tt), grid=(n // tt,), num_prefetch=1,
        in_specs=[
            pl.BlockSpec((tt, d), lambda i, p: (i, 0)),
            pl.BlockSpec(memory_space=pl.ANY),
        ],
        out_specs=pl.BlockSpec((tt, d), lambda i, p: (i, 0)),
        out_shape=jax.ShapeDtypeStruct((n, d), F32),
        scratch=[pltpu.VMEM((2, TOP_K, tt, d), F32), pltpu.SemaphoreType.DMA((2,))],
        semantics=("arbitrary",),
        name="moe_combine",
    )(pos, x, y_rows)


def _short_conv_mixer(x, g, w_in, conv_w, w_out, seq):
    d = x.shape[1]
    (y,) = _inproj(x, g, w_in.astype(BF16), conv_w[None], jnp.zeros((1, 1, d), F32), hyena=False, seq=seq)
    return _matmul_res(y, w_out.astype(BF16), x, "shortconv_out")


def _hyena_mixer(x, g, w_in, conv_w, conv_b, f_w1, f_b1, f_w_hid, f_b_hid, f_freq, f_w_out, skip, w_out,
                 batch, seq):
    d = x.shape[1]
    cw = conv_w.reshape(3, 3, d).transpose(1, 0, 2)
    cb = conv_b.reshape(3, 1, d)
    x0, vv = _inproj(x, g, w_in.astype(BF16), cw, cb, hyena=True, seq=seq)
    filt = _hyena_filters(seq, d, f_w1, f_b1, f_w_hid, f_b_hid, f_freq, f_w_out)
    y = _hyena_long_conv(vv, x0, filt, skip[None], batch, seq)
    return _matmul_res(y, w_out.astype(BF16), x, "hyena_out")


def kernel(x, norm_mix, norm_ffn, sc_w_in, sc_conv_w, sc_w_out, hy_w_in, hy_conv_w, hy_conv_b, hy_filt_w1, hy_filt_b1, hy_filt_w_hid, hy_filt_b_hid, hy_filt_freq, hy_filt_w_out, hy_skip, hy_w_out, mla_w_down, mla_q_norm, mla_kv_norm, mla_w_uq, mla_w_ukv, mla_q_gain, mla_k_gain, mla_w_o, router_w_group, router_b_group, router_w_expert, router_b_expert, moe_w_up, moe_w_down):
    batch, seq, d = x.shape
    depth = norm_mix.shape[0]
    n_mixers = 3
    xf = x.reshape(batch * seq, d)
    for i in range(depth):
        m = i % n_mixers
        j = i // n_mixers
        g = norm_mix[i][None]
        if m == 0:
            xf = _short_conv_mixer(xf, g, sc_w_in[j], sc_conv_w[j], sc_w_out[j], seq)
        elif m == 1:
            xf = _hyena_mixer(xf, g, hy_w_in[j], hy_conv_w[j], hy_conv_b[j], hy_filt_w1[j], hy_filt_b1[j],
                              hy_filt_w_hid[j], hy_filt_b_hid[j], hy_filt_freq[j], hy_filt_w_out[j],
                              hy_skip[j], hy_w_out[j], batch, seq)
        else:
            xf = _mla_mixer(xf, g, mla_w_down[j], mla_q_norm[j], mla_kv_norm[j], mla_w_uq[j], mla_w_ukv[j],
                            mla_q_gain[j], mla_k_gain[j], mla_w_o[j], batch, seq)
        xf = _hier_moe(xf, norm_ffn[i][None], router_w_group[i], router_b_group[i], router_w_expert[i],
                       router_b_expert[i], moe_w_up[i], moe_w_down[i])
    return xf.reshape(batch, seq, d)
```

```python
import functools
import math

import jax
import jax.numpy as jnp
import numpy as np
from jax import lax
from jax.experimental import pallas as pl
from jax.experimental.pallas import tpu as pltpu

F32 = jnp.float32
BF16 = jnp.bfloat16
I32 = jnp.int32
U32 = jnp.uint32
HIGH_HALF = np.uint32(0xFFFF0000)
EPS = 1e-6

LANE = 128
HALO = 16
VMEM_LIMIT_BYTES = 56 * 1024 * 1024

D_NOPE = 128
D_ROPE = 64
D_V = 128
D_QK = D_NOPE + D_ROPE
D_QK_PAD = 256
D_V_PAD = 256
ROPE_THETA = 10000.0
TOP_K = 2
HY_FAST_DECAY = 0.3
HY_SLOW_DECAY = 1.5
HY_TARGET = 1e-2

HIGHEST = lax.Precision.HIGHEST


def _tile(dim, want):
    t = min(dim, want)
    assert dim % t == 0, (dim, want)
    return t


def _call(kernel, *, grid, in_specs, out_specs, out_shape, semantics, name, scratch=(), num_prefetch=0):
    grid_spec = pltpu.PrefetchScalarGridSpec(
        num_scalar_prefetch=num_prefetch, grid=grid, in_specs=in_specs, out_specs=out_specs,
        scratch_shapes=list(scratch))
    return pl.pallas_call(
        kernel, grid_spec=grid_spec, out_shape=out_shape, name=name,
        compiler_params=pltpu.CompilerParams(dimension_semantics=semantics, vmem_limit_bytes=VMEM_LIMIT_BYTES))


def _dot(a, b, precision=None):
    return jnp.dot(a, b, preferred_element_type=F32, precision=precision)


def _rms(x, g):
    return x * lax.rsqrt(jnp.mean(x * x, axis=-1, keepdims=True) + EPS) * g


def _pack_rows(vals):
    half = vals.shape[1] // 2
    lo_bits = pltpu.bitcast(vals[:, :half].astype(BF16).astype(F32), U32)
    hi_bits = pltpu.bitcast(vals[:, half:].astype(BF16).astype(F32), U32)
    return (lo_bits >> 16) | (hi_bits & HIGH_HALF)


def _unpack_rows(words):
    return pltpu.bitcast(words << 16, F32), pltpu.bitcast(words & HIGH_HALF, F32)


def _conv3_rows(z, cw, lo, hi, tm):
    rows = z.shape[0]
    r = lax.broadcasted_iota(I32, z.shape, 0)
    z = jnp.where((r < lo) | (r >= hi), 0.0, z)
    zm = pltpu.roll(z, 1, 0)
    zp = pltpu.roll(z, rows - 1, 0)
    y = zm * cw[0:1] + z * cw[1:2] + zp * cw[2:3]
    return y[HALO:HALO + tm]


def _inproj_kernel(x_ref, xp_ref, xn_ref, g_ref, wa_ref, wb_ref, wc_ref, cw_ref, cb_ref, *rest,
                   hyena, seq_tiles, tm):
    hs_ref = rest[-1]
    i = pl.program_id(0)

    @pl.when(pl.program_id(1) == 0)
    def _():
        g = g_ref[...]
        hs_ref[0:HALO, :] = _rms(xp_ref[...], g).astype(BF16)
        hs_ref[HALO:HALO + tm, :] = _rms(x_ref[...], g).astype(BF16)
        hs_ref[HALO + tm:, :] = _rms(xn_ref[...], g).astype(BF16)

    it = i % seq_tiles
    lo = jnp.where(it == 0, HALO, 0)
    hi = jnp.where(it == seq_tiles - 1, HALO + tm, tm + 2 * HALO)
    hs = hs_ref[...]
    za = _dot(hs, wa_ref[...])
    zb = _dot(hs, wb_ref[...])
    zc = _dot(hs, wc_ref[...])
    if hyena:
        x0_ref, vv_ref = rest[0], rest[1]
        x0 = _conv3_rows(za, cw_ref[0], lo, hi, tm) + cb_ref[0]
        x1 = _conv3_rows(zb, cw_ref[1], lo, hi, tm) + cb_ref[1]
        v = _conv3_rows(zc, cw_ref[2], lo, hi, tm) + cb_ref[2]
        x0_ref[...] = x0.astype(BF16)
        vv_ref[...] = (v * x1).astype(BF16)
    else:
        y_ref = rest[0]
        y = za[HALO:HALO + tm] * _conv3_rows(zb * zc, cw_ref[0], lo, hi, tm)
        y_ref[...] = y.astype(BF16)


def _inproj(x, g, w_in, layer, cw, cb, *, hyena, seq):
    n, d = x.shape
    tm = _tile(seq, 512)
    tn = _tile(d, 512)
    nj = d // tn
    hb = tm // HALO
    last_hb = n // HALO - 1
    groups = cw.shape[0]
    kern = functools.partial(_inproj_kernel, hyena=hyena, seq_tiles=seq // tm, tm=tm)
    n_out = 2 if hyena else 1
    out_spec = pl.BlockSpec((tm, tn), lambda i, j: (i, j))
    outs = _call(
        kern, grid=(n // tm, nj),
        in_specs=[
            pl.BlockSpec((tm, d), lambda i, j: (i, 0)),
            pl.BlockSpec((HALO, d), lambda i, j: (jnp.maximum(i * hb - 1, 0), 0)),
            pl.BlockSpec((HALO, d), lambda i, j: (jnp.minimum((i + 1) * hb, last_hb), 0)),
            pl.BlockSpec((1, d), lambda i, j: (0, 0)),
            pl.BlockSpec((None, d, tn), lambda i, j: (layer, 0, j)),
            pl.BlockSpec((None, d, tn), lambda i, j: (layer, 0, nj + j)),
            pl.BlockSpec((None, d, tn), lambda i, j: (layer, 0, 2 * nj + j)),
            pl.BlockSpec((groups, 3, tn), lambda i, j: (0, 0, j)),
            pl.BlockSpec((groups, 1, tn), lambda i, j: (0, 0, j)),
        ],
        out_specs=[out_spec] * n_out,
        out_shape=[jax.ShapeDtypeStruct((n, d), BF16)] * n_out,
        scratch=[pltpu.VMEM((tm + 2 * HALO, d), BF16)],
        semantics=("parallel", "arbitrary"),
        name="hyena_inproj" if hyena else "shortconv_inproj",
    )(x, x, x, g, w_in, w_in, w_in, cw, cb)
    return outs


def _matmul_res_kernel(a_ref, w_ref, r_ref, o_ref):
    o_ref[...] = r_ref[...] + _dot(a_ref[...], w_ref[...])


def _matmul_res(a, w, layer, res, name):
    n, k = a.shape
    dout = w.shape[2]
    tm = _tile(n, 1024)
    tn = _tile(dout, 1024)
    return _call(
        _matmul_res_kernel, grid=(n // tm, dout // tn),
        in_specs=[
            pl.BlockSpec((tm, k), lambda i, j: (i, 0)),
            pl.BlockSpec((None, k, tn), lambda i, j: (layer, 0, j)),
            pl.BlockSpec((tm, tn), lambda i, j: (i, j)),
        ],
        out_specs=pl.BlockSpec((tm, tn), lambda i, j: (i, j)),
        out_shape=jax.ShapeDtypeStruct((n, dout), F32),
        semantics=("parallel", "parallel"),
        name=name,
    )(a, w, res)


def _filter_kernel(z_ref, w1_ref, b1_ref, wh_ref, bh_ref, fr_ref, wo_ref, dl_ref, o_ref, h_ref, *, n_hid, n_half):
    j = pl.program_id(0)

    @pl.when(j == 0)
    def _():
        h = jnp.sin(fr_ref[0:1, :] * (_dot(z_ref[...], w1_ref[...], HIGHEST) + b1_ref[...]))
        for l in range(n_hid):
            h = jnp.sin(fr_ref[l + 1:l + 2, :] * (_dot(h, wh_ref[l], HIGHEST) + bh_ref[l:l + 1, :]))
        h_ref[...] = h

    f = _dot(h_ref[...], wo_ref[...], HIGHEST)
    t = z_ref[:, 0:1]
    f = f * jnp.exp(-t * dl_ref[...])
    row = lax.broadcasted_iota(I32, f.shape, 0)
    f = jnp.where(row < jnp.where(j >= n_half, 1, 0), 0.0, f)
    o_ref[...] = f.astype(BF16)


def _pad_to(a, shape):
    return jnp.pad(a, [(0, s - d) for s, d in zip(shape, a.shape)])


def _hyena_filters(seq, d, f_w1, f_b1, f_w_hid, f_b_hid, f_freq, f_w_out):
    emb, fh = f_w1.shape
    n_hid = f_w_hid.shape[0]
    bands_n = (emb - 1) // 2
    t = jnp.linspace(0.0, 1.0, seq, dtype=F32)[:, None]
    bands = jnp.linspace(1e-4, bands_n - 1, bands_n, dtype=F32)[None]
    w = 2.0 * math.pi * jnp.arange(seq, dtype=F32)[:, None] / seq
    z = jnp.concatenate([t, jnp.cos(bands * w), -jnp.sin(bands * w)], axis=-1)
    max_decay = math.log(HY_TARGET) / HY_FAST_DECAY
    min_decay = math.log(HY_TARGET) / HY_SLOW_DECAY
    deltas = jnp.abs(jnp.linspace(min_decay, max_decay, d, dtype=F32))[None]
    deltas2 = jnp.concatenate([deltas, deltas], axis=-1)
    tn = _tile(d, 512)
    kern = functools.partial(_filter_kernel, n_hid=n_hid, n_half=d // tn)
    full = lambda shape: pl.BlockSpec(shape, lambda j: (0,) * len(shape))
    return _call(
        kern, grid=(2 * d // tn,),
        in_specs=[
            full((seq, LANE)), full((LANE, LANE)), full((1, LANE)), full((n_hid, LANE, LANE)),
            full((n_hid, LANE)), full((n_hid + 1, LANE)),
            pl.BlockSpec((LANE, tn), lambda j: (0, j)),
            pl.BlockSpec((1, tn), lambda j: (0, j)),
        ],
        out_specs=pl.BlockSpec((seq, tn), lambda j: (0, j)),
        out_shape=jax.ShapeDtypeStruct((seq, 2 * d), BF16),
        scratch=[pltpu.VMEM((seq, LANE), F32)],
        semantics=("arbitrary",),
        name="hyena_filter",
    )(_pad_to(z, (seq, LANE)), _pad_to(f_w1, (LANE, LANE)), _pad_to(f_b1[None], (1, LANE)),
      _pad_to(f_w_hid, (n_hid, LANE, LANE)), _pad_to(f_b_hid, (n_hid, LANE)), _pad_to(f_freq, (n_hid + 1, LANE)),
      _pad_to(f_w_out, (LANE, 2 * d)), deltas2)


def _dft_matrices(seq):
    n = 2 * seq
    k = jnp.arange(seq, dtype=I32)

    def cos_sin(freqs):
        ang = ((freqs[:, None] * k[None, :]) % n).astype(F32) * (2.0 * math.pi / n)
        return jnp.cos(ang), jnp.sin(ang)

    rows = _tile(seq, 512)
    blocks = seq // rows
    c0, s0 = cos_sin(jnp.arange(blocks, dtype=I32) * rows)
    c1, s1 = cos_sin(jnp.arange(rows, dtype=I32))
    c0, s0, c1, s1 = lax.optimization_barrier((c0, s0, c1, s1))
    c = (c0[:, None, :] * c1[None] - s0[:, None, :] * s1[None]).reshape(seq, seq)
    s = (s0[:, None, :] * c1[None] + c0[:, None, :] * s1[None]).reshape(seq, seq)
    alt = jnp.where(k % 2 == 0, 1.0, -1.0).astype(F32)
    wc = c
    ws = jnp.where(k[:, None] == 0, alt[None, :], -s)
    wk = jnp.where(k == 0, 1.0, 2.0).astype(F32) / n
    gc = c * wk[None, :]
    gs = jnp.where(k[None, :] == 0, alt[:, None] / n, -s * wk[None, :])
    return wc.astype(BF16), ws.astype(BF16), gc.astype(BF16), gs.astype(BF16)


def _dc_rows(shape, freq_tile):
    row = lax.broadcasted_iota(I32, shape, 0)
    return row < jnp.where(freq_tile == 0, 1, 0)


def _filter_spec_kernel(wc_ref, ws_ref, hf_ref, hb_ref, ha_ref, hbo_ref):
    wc = wc_ref[...]
    ws = ws_ref[...]
    hf = hf_ref[...]
    hb = hb_ref[...]
    ha_ref[...] = _dot(wc, hf) + _dot(wc, hb)
    bf = _dot(ws, hf)
    bb = _dot(ws, hb)
    hbo_ref[...] = jnp.where(_dc_rows(bf.shape, pl.program_id(0)), bf + bb, bf - bb)


def _signal_spec_kernel(wc_ref, ws_ref, v_ref, ha_ref, hb_ref, ya_ref, yb_ref):
    v = v_ref[...]
    a = _dot(wc_ref[...], v)
    b = _dot(ws_ref[...], v)
    ha = ha_ref[...]
    hb = hb_ref[...]
    dc = _dc_rows(a.shape, pl.program_id(0))
    ya_ref[...] = jnp.where(dc, a * ha, a * ha - b * hb).astype(BF16)
    yb_ref[...] = jnp.where(dc, b * hb, a * hb + b * ha).astype(BF16)


def _inverse_kernel(gc_ref, gs_ref, ya_ref, yb_ref, vv_ref, x0_ref, sk_ref, o_ref):
    y = _dot(gc_ref[...], ya_ref[...]) + _dot(gs_ref[...], yb_ref[...])
    y = y + vv_ref[...].astype(F32) * sk_ref[...]
    o_ref[...] = (y * x0_ref[...].astype(F32)).astype(BF16)


def _hyena_long_conv(vv, x0, filt, skip, batch, seq):
    d = vv.shape[1]
    wc, ws, gc, gs = _dft_matrices(seq)
    tk = _tile(seq, 512)
    tn = _tile(d, 512)
    nj = d // tn
    ha, hb = _call(
        _filter_spec_kernel, grid=(seq // tk, nj),
        in_specs=[
            pl.BlockSpec((tk, seq), lambda kt, j: (kt, 0)),
            pl.BlockSpec((tk, seq), lambda kt, j: (kt, 0)),
            pl.BlockSpec((seq, tn), lambda kt, j: (0, j)),
            pl.BlockSpec((seq, tn), lambda kt, j: (0, nj + j)),
        ],
        out_specs=[pl.BlockSpec((tk, tn), lambda kt, j: (kt, j))] * 2,
        out_shape=[jax.ShapeDtypeStruct((seq, d), F32)] * 2,
        semantics=("parallel", "parallel"),
        name="hyena_filter_spectrum",
    )(wc, ws, filt, filt)

    v3 = vv.reshape(batch, seq, d)
    ya, yb = _call(
        _signal_spec_kernel, grid=(seq // tk, batch, nj),
        in_specs=[
            pl.BlockSpec((tk, seq), lambda kt, b, j: (kt, 0)),
            pl.BlockSpec((tk, seq), lambda kt, b, j: (kt, 0)),
            pl.BlockSpec((None, seq, tn), lambda kt, b, j: (b, 0, j)),
            pl.BlockSpec((tk, tn), lambda kt, b, j: (kt, j)),
            pl.BlockSpec((tk, tn), lambda kt, b, j: (kt, j)),
        ],
        out_specs=[pl.BlockSpec((None, tk, tn), lambda kt, b, j: (b, kt, j))] * 2,
        out_shape=[jax.ShapeDtypeStruct((batch, seq, d), BF16)] * 2,
        semantics=("parallel", "parallel", "parallel"),
        name="hyena_signal_spectrum",
    )(wc, ws, v3, ha, hb)

    tt = _tile(seq, 512)
    out = _call(
        _inverse_kernel, grid=(seq // tt, batch, nj),
        in_specs=[
            pl.BlockSpec((tt, seq), lambda ti, b, j: (ti, 0)),
            pl.BlockSpec((tt, seq), lambda ti, b, j: (ti, 0)),
            pl.BlockSpec((None, seq, tn), lambda ti, b, j: (b, 0, j)),
            pl.BlockSpec((None, seq, tn), lambda ti, b, j: (b, 0, j)),
            pl.BlockSpec((None, tt, tn), lambda ti, b, j: (b, ti, j)),
            pl.BlockSpec((None, tt, tn), lambda ti, b, j: (b, ti, j)),
            pl.BlockSpec((1, tn), lambda ti, b, j: (0, j)),
        ],
        out_specs=pl.BlockSpec((None, tt, tn), lambda ti, b, j: (b, ti, j)),
        out_shape=jax.ShapeDtypeStruct((batch, seq, d), BF16),
        semantics=("parallel", "parallel", "parallel"),
        name="hyena_inverse",
    )(gc, gs, ya, yb, v3, x0.reshape(batch, seq, d), skip)
    return out.reshape(batch * seq, d)


def _mla_down_kernel(x_ref, g_ref, w_ref, qn_ref, kvn_ref, cq_ref, ckv_ref, kr_ref, *, q_rank, kv_rank):
    h = _rms(x_ref[...], g_ref[...]).astype(BF16)
    d = _dot(h, w_ref[...])
    cq_ref[...] = _rms(d[:, :q_rank], qn_ref[...]).astype(BF16)
    ckv_ref[...] = _rms(d[:, q_rank:q_rank + kv_rank], kvn_ref[...]).astype(BF16)
    kr_ref[...] = d[:, q_rank + kv_rank:]


def _rope_head(nope, rope, rope_swapped, gains, cos, sin, scale):
    ms = (jnp.sum(nope * nope, axis=-1, keepdims=True) + jnp.sum(rope * rope, axis=-1, keepdims=True)) / D_QK
    r = lax.rsqrt(ms + EPS) * scale
    out_nope = nope * r * gains[:, :LANE]
    out_rope = r * (rope * gains[:, LANE:2 * LANE] * cos + rope_swapped * gains[:, 2 * LANE:] * sin)
    return out_nope, out_rope


def _mla_q_kernel(cq_ref, w_ref, gn_ref, cos_ref, sin_ref, q_ref, *, heads, scale):
    z = _dot(cq_ref[...], w_ref[...])
    cos = cos_ref[...]
    sin = sin_ref[...]
    gains = gn_ref[...]
    for h in range(heads):
        zh = z[:, h * 3 * LANE:(h + 1) * 3 * LANE]
        qn, qr = _rope_head(zh[:, :LANE], zh[:, LANE:2 * LANE], zh[:, 2 * LANE:], gains, cos, sin, scale)
        q_ref[:, h * D_QK_PAD:h * D_QK_PAD + LANE] = qn.astype(BF16)
        q_ref[:, h * D_QK_PAD + LANE:(h + 1) * D_QK_PAD] = qr.astype(BF16)


def _mla_kv_kernel(ckv_ref, w_ref, kr_ref, gn_ref, cos_ref, sin_ref, k_ref, v_ref, *, heads):
    z = _dot(ckv_ref[...], w_ref[...])
    kr = kr_ref[...]
    cos = cos_ref[...]
    sin = sin_ref[...]
    gains = gn_ref[...]
    lane = lax.broadcasted_iota(I32, (z.shape[0], LANE), 1)
    ones_col = jnp.where(lane == 0, 1.0, 0.0).astype(BF16)
    for h in range(heads):
        zh = z[:, h * 2 * LANE:(h + 1) * 2 * LANE]
        kn, krot = _rope_head(zh[:, :LANE], kr[:, :LANE], kr[:, LANE:], gains, cos, sin, 1.0)
        k_ref[:, h * D_QK_PAD:h * D_QK_PAD + LANE] = kn.astype(BF16)
        k_ref[:, h * D_QK_PAD + LANE:(h + 1) * D_QK_PAD] = krot.astype(BF16)
        v_ref[:, h * D_V_PAD:h * D_V_PAD + D_V] = zh[:, LANE:].astype(BF16)
        v_ref[:, h * D_V_PAD + D_V:(h + 1) * D_V_PAD] = ones_col


def _attention_kernel(q_ref, k_ref, v_ref, o_ref, acc_ref, *, tk):
    q = q_ref[...]
    m_old = None
    for c in range(k_ref.shape[0] // tk):
        rows = slice(c * tk, (c + 1) * tk)
        s = lax.dot_general(q, k_ref[rows, :], (((1,), (1,)), ((), ())), preferred_element_type=F32)
        m_new = jnp.max(s, axis=-1, keepdims=True)
        if m_old is not None:
            m_new = jnp.maximum(m_old, m_new)
        pv = _dot(jnp.exp(s - m_new).astype(BF16), v_ref[rows, :])
        if m_old is None:
            acc_ref[...] = pv
        else:
            acc_ref[...] = jnp.exp(m_old - m_new) * acc_ref[...] + pv
        m_old = m_new
    acc = acc_ref[...]
    o_ref[...] = (acc[:, :D_V] * (1.0 / acc[:, D_V:D_V + 1])).astype(BF16)


def _swap_halves(a):
    half = a.shape[-1] // 2
    return jnp.concatenate([a[..., half:], a[..., :half]], axis=-1)


def _lane_pad(a):
    return jnp.pad(a, [(0, 0)] * (a.ndim - 1) + [(0, LANE - a.shape[-1])])


def _rope_gains(gain):
    rope = gain[D_NOPE:]
    return jnp.concatenate([gain[:D_NOPE], _lane_pad(rope), _lane_pad(_swap_halves(rope))])[None]


def _mla_mixer(x, g, w_down, q_norm, kv_norm, w_uq, w_ukv, q_gain, k_gain, w_o, batch, seq):
    n, d = x.shape
    q_rank = q_norm.shape[0]
    kv_rank = kv_norm.shape[0]
    heads = w_o.shape[0] // D_V

    w_kr = w_down[:, q_rank + kv_rank:]
    w_down_p = jnp.concatenate(
        [w_down[:, :q_rank + kv_rank], _lane_pad(w_kr), _lane_pad(_swap_halves(w_kr))], axis=-1).astype(BF16)
    wq = w_uq.reshape(q_rank, heads, D_QK)
    wq_rope = wq[..., D_NOPE:]
    wq_p = jnp.concatenate([wq[..., :D_NOPE], _lane_pad(wq_rope), _lane_pad(_swap_halves(wq_rope))], axis=-1)
    wq_p = wq_p.reshape(q_rank, heads * 3 * LANE).astype(BF16)
    wkv = w_ukv.astype(BF16)

    inv = ROPE_THETA ** (-jnp.arange(0, D_ROPE, 2, dtype=F32) / D_ROPE)
    ang = jnp.arange(seq, dtype=F32)[:, None] * inv[None]
    cos = _lane_pad(jnp.concatenate([jnp.cos(ang), jnp.cos(ang)], axis=-1))
    sin = _lane_pad(jnp.concatenate([-jnp.sin(ang), jnp.sin(ang)], axis=-1))

    tm = _tile(seq, 512)
    seq_tiles = seq // tm
    dw = w_down_p.shape[1]
    cq, ckv, kr = _call(
        functools.partial(_mla_down_kernel, q_rank=q_rank, kv_rank=kv_rank), grid=(n // tm,),
        in_specs=[
            pl.BlockSpec((tm, d), lambda i: (i, 0)),
            pl.BlockSpec((1, d), lambda i: (0, 0)),
            pl.BlockSpec((d, dw), lambda i: (0, 0)),
            pl.BlockSpec((1, q_rank), lambda i: (0, 0)),
            pl.BlockSpec((1, kv_rank), lambda i: (0, 0)),
        ],
        out_specs=[
            pl.BlockSpec((tm, q_rank), lambda i: (i, 0)),
            pl.BlockSpec((tm, kv_rank), lambda i: (i, 0)),
            pl.BlockSpec((tm, 2 * LANE), lambda i: (i, 0)),
        ],
        out_shape=[
            jax.ShapeDtypeStruct((n, q_rank), BF16),
            jax.ShapeDtypeStruct((n, kv_rank), BF16),
            jax.ShapeDtypeStruct((n, 2 * LANE), F32),
        ],
        semantics=("parallel",),
        name="mla_down",
    )(x, g, w_down_p, q_norm[None], kv_norm[None])

    hb = _tile(heads, 4)
    table_spec = pl.BlockSpec((tm, LANE), lambda i, j: (i % seq_tiles, 0))
    q = _call(
        functools.partial(_mla_q_kernel, heads=hb, scale=D_QK ** -0.5), grid=(n // tm, heads // hb),
        in_specs=[
            pl.BlockSpec((tm, q_rank), lambda i, j: (i, 0)),
            pl.BlockSpec((q_rank, hb * 3 * LANE), lambda i, j: (0, j)),
            pl.BlockSpec((1, 3 * LANE), lambda i, j: (0, 0)),
            table_spec, table_spec,
        ],
        out_specs=pl.BlockSpec((tm, hb * D_QK_PAD), lambda i, j: (i, j)),
        out_shape=jax.ShapeDtypeStruct((n, heads * D_QK_PAD), BF16),
        semantics=("parallel", "parallel"),
        name="mla_q",
    )(cq, wq_p, _rope_gains(q_gain), cos, sin)

    k, v = _call(
        functools.partial(_mla_kv_kernel, heads=hb), grid=(n // tm, heads // hb),
        in_specs=[
            pl.BlockSpec((tm, kv_rank), lambda i, j: (i, 0)),
            pl.BlockSpec((kv_rank, hb * 2 * LANE), lambda i, j: (0, j)),
            pl.BlockSpec((tm, 2 * LANE), lambda i, j: (i, 0)),
            pl.BlockSpec((1, 3 * LANE), lambda i, j: (0, 0)),
            table_spec, table_spec,
        ],
        out_specs=[
            pl.BlockSpec((tm, hb * D_QK_PAD), lambda i, j: (i, j)),
            pl.BlockSpec((tm, hb * D_V_PAD), lambda i, j: (i, j)),
        ],
        out_shape=[
            jax.ShapeDtypeStruct((n, heads * D_QK_PAD), BF16),
            jax.ShapeDtypeStruct((n, heads * D_V_PAD), BF16),
        ],
        semantics=("parallel", "parallel"),
        name="mla_kv",
    )(ckv, wkv, kr, _rope_gains(k_gain), cos, sin)

    tq = _tile(seq, 1024)
    qt = seq // tq
    o = _call(
        functools.partial(_attention_kernel, tk=_tile(seq, 512)), grid=(batch, heads, qt),
        in_specs=[
            pl.BlockSpec((tq, D_QK_PAD), lambda b, h, i: (b * qt + i, h)),
            pl.BlockSpec((seq, D_QK_PAD), lambda b, h, i: (b, h)),
            pl.BlockSpec((seq, D_V_PAD), lambda b, h, i: (b, h)),
        ],
        out_specs=pl.BlockSpec((tq, D_V), lambda b, h, i: (b * qt + i, h)),
        out_shape=jax.ShapeDtypeStruct((n, heads * D_V), BF16),
        scratch=[pltpu.VMEM((tq, D_V_PAD), F32)],
        semantics=("parallel", "parallel", "parallel"),
        name="mla_attention",
    )(q, k, v)
    return _matmul_res(o, w_o.astype(BF16)[None], 0, x, "mla_out")


def _router_kernel(x_ref, g_ref, w_ref, b_ref, hp_ref, ids_ref, gates_ref, *, n_groups, per_group):
    h = _rms(x_ref[...], g_ref[...])
    hp_ref[...] = _pack_rows(h)
    logits = _dot(h, w_ref[...], HIGHEST) + b_ref[...]
    lane = lax.broadcasted_iota(I32, logits.shape, 1)
    neg = -jnp.inf

    def first_argmax(vals, vmax):
        return jnp.min(jnp.where(vals == vmax, lane, LANE), axis=-1, keepdims=True)

    gl = jnp.where(lane < n_groups, logits, neg)
    gmax = jnp.max(gl, axis=-1, keepdims=True)
    g_sel = first_argmax(gl, gmax)
    g_prob = 1.0 / jnp.sum(jnp.exp(gl - gmax), axis=-1, keepdims=True)
    lo = n_groups + g_sel * per_group
    el = jnp.where((lane >= lo) & (lane < lo + per_group), logits, neg)
    t1 = jnp.max(el, axis=-1, keepdims=True)
    i1 = first_argmax(el, t1)
    el2 = jnp.where(lane == i1, neg, el)
    t2 = jnp.max(el2, axis=-1, keepdims=True)
    i2 = first_argmax(el2, t2)
    d = jnp.exp(t2 - t1)
    p1 = 1.0 / (1.0 + d)
    ids_ref[...] = jnp.where(lane == 0, i1 - n_groups, jnp.where(lane == 1, i2 - n_groups, 0))
    gates_ref[...] = jnp.where(lane == 0, g_prob * p1, jnp.where(lane == 1, g_prob * (d * p1), 0.0))


def _start_row_gather(src_hbm, rows_ref, first, dst, sem, count, unrolled):
    def start(r):
        pltpu.make_async_copy(src_hbm.at[pl.ds(rows_ref[first + r], 1)], dst.at[pl.ds(r, 1)], sem).start()

    if unrolled:
        for r in range(count):
            start(r)
    else:
        def body(r, carry):
            start(r)
            return carry

        lax.fori_loop(0, count, body, 0)


def _wait_row_gather(dst, sem):
    pltpu.make_async_copy(dst, dst, sem).wait()


def _expert_kernel(block_e_ref, src0_ref, stok_ref, hp_hbm, wup_ref, wdn_ref, y_ref,
                   xbuf, xs_ref, wup_bf, wdn_bf, sem, *, tm, d_ff):
    i = pl.program_id(0)
    last = pl.num_programs(0) - 1
    slot = i % 2

    @pl.when(i == 0)
    def _():
        _start_row_gather(hp_hbm, stok_ref, src0_ref[0], xbuf.at[0], sem.at[0], tm, unrolled=False)

    _wait_row_gather(xbuf.at[slot], sem.at[slot])

    @pl.when((i == 0) | (block_e_ref[i] != block_e_ref[jnp.maximum(i - 1, 0)]))
    def _():
        wup_bf[...] = wup_ref[...].astype(BF16)
        wdn_bf[...] = wdn_ref[...].astype(BF16)

    lo, hi = _unpack_rows(xbuf[slot])
    half = lo.shape[1]
    xs_ref[:, :half] = lo.astype(BF16)
    xs_ref[:, half:] = hi.astype(BF16)

    nxt = jnp.minimum(i + 1, last)
    _start_row_gather(hp_hbm, stok_ref, src0_ref[nxt], xbuf.at[1 - slot], sem.at[1 - slot], tm, unrolled=True)

    hgu = _dot(xs_ref[...], wup_bf[...])
    hg = hgu[:, :d_ff]
    hu = hgu[:, d_ff:]
    act = (hg * jax.nn.sigmoid(hg) * hu).astype(BF16)
    y_ref[...] = _pack_rows(_dot(act, wdn_bf[...]))

    @pl.when(i == last)
    def _():
        _wait_row_gather(xbuf.at[1 - slot], sem.at[1 - slot])


def _combine_kernel(pos_ref, x_ref, gates_ref, y_hbm, o_ref, ybuf, sem, *, tt):
    i = pl.program_id(0)
    last = pl.num_programs(0) - 1
    slot = i % 2
    count = TOP_K * tt

    @pl.when(i == 0)
    def _():
        _start_row_gather(y_hbm, pos_ref, 0, ybuf.at[0], sem.at[0], count, unrolled=False)

    _wait_row_gather(ybuf.at[slot], sem.at[slot])
    nxt = jnp.minimum(i + 1, last)
    _start_row_gather(y_hbm, pos_ref, nxt * count, ybuf.at[1 - slot], sem.at[1 - slot], count, unrolled=True)

    gates = gates_ref[...]
    half = x_ref.shape[1] // 2
    acc_lo = x_ref[:, :half]
    acc_hi = x_ref[:, half:]
    for k in range(TOP_K):
        lo, hi = _unpack_rows(ybuf[slot, k * tt:(k + 1) * tt, :])
        acc_lo = acc_lo + gates[:, k:k + 1] * lo
        acc_hi = acc_hi + gates[:, k:k + 1] * hi
    o_ref[:, :half] = acc_lo
    o_ref[:, half:] = acc_hi

    @pl.when(i == last)
    def _():
        _wait_row_gather(ybuf.at[1 - slot], sem.at[1 - slot])


def _moe_plan(ids, n_experts, tm, tt):
    n = ids.shape[0]
    a = n * TOP_K
    experts = jnp.arange(n_experts, dtype=I32)
    e2 = ids.reshape(a // LANE, LANE)
    member = e2[None] == experts[:, None, None]
    counts = jnp.sum(member.astype(I32), axis=(1, 2))
    starts = jnp.cumsum(counts) - counts
    padded = (counts + tm - 1) // tm * tm
    pend = jnp.cumsum(padded)
    pstart = pend - padded
    order = jnp.argsort(ids.reshape(a), stable=True).astype(I32)
    rank_sorted = jnp.argsort(order).astype(I32)
    shift = jnp.sum(jnp.where(member, (pstart - starts)[:, None, None], 0), axis=0).reshape(a)
    pos = (rank_sorted + shift).reshape(n // tt, tt, TOP_K).transpose(0, 2, 1).reshape(a)
    n_tiles = (a + n_experts * tm) // tm
    tile_row = jnp.arange(n_tiles, dtype=I32) * tm
    block_e = jnp.minimum(jnp.sum((pend[None, :] <= tile_row[:, None]).astype(I32), axis=1), n_experts - 1)
    pick = block_e[:, None] == experts[None, :]
    tile_shift = jnp.sum(jnp.where(pick, (starts - pstart)[None, :], 0), axis=1)
    src0 = jnp.clip(tile_row + tile_shift, 0, a)
    stok = jnp.concatenate([order // TOP_K, jnp.zeros((tm,), I32)])
    return block_e, src0, stok, pos


def _hier_moe(x, g, w_group, b_group, w_expert, b_expert, w_up, w_down, layer):
    n, d = x.shape
    n_groups = w_group.shape[1]
    n_experts = w_expert.shape[1]
    d_ff = w_down.shape[2]
    half = d // 2
    w_r = _pad_to(jnp.concatenate([w_group, w_expert], axis=1), (d, LANE))
    b_r = _pad_to(jnp.concatenate([b_group, b_expert])[None], (1, LANE))
    tr = _tile(n, 512)
    hp, ids, gates = _call(
        functools.partial(_router_kernel, n_groups=n_groups, per_group=n_experts // n_groups), grid=(n // tr,),
        in_specs=[
            pl.BlockSpec((tr, d), lambda i: (i, 0)),
            pl.BlockSpec((1, d), lambda i: (0, 0)),
            pl.BlockSpec((d, LANE), lambda i: (0, 0)),
            pl.BlockSpec((1, LANE), lambda i: (0, 0)),
        ],
        out_specs=[
            pl.BlockSpec((tr, half), lambda i: (i, 0)),
            pl.BlockSpec((tr, LANE), lambda i: (i, 0)),
            pl.BlockSpec((tr, LANE), lambda i: (i, 0)),
        ],
        out_shape=[
            jax.ShapeDtypeStruct((n, half), U32),
            jax.ShapeDtypeStruct((n, LANE), I32),
            jax.ShapeDtypeStruct((n, LANE), F32),
        ],
        semantics=("parallel",),
        name="moe_router",
    )(x, g, w_r, b_r)

    tm = 256 if n * TOP_K >= 256 * n_experts else 128
    tt = _tile(n, 128)
    block_e, src0, stok, pos = _moe_plan(ids[:, :TOP_K], n_experts, tm, tt)
    n_rows = n * TOP_K + n_experts * tm
    y_rows = _call(
        functools.partial(_expert_kernel, tm=tm, d_ff=d_ff), grid=(n_rows // tm,), num_prefetch=3,
        in_specs=[
            pl.BlockSpec(memory_space=pl.ANY),
            pl.BlockSpec((None, None, d, 2 * d_ff), lambda i, be, s0, st: (layer, be[i], 0, 0)),
            pl.BlockSpec((None, None, d_ff, d), lambda i, be, s0, st: (layer, be[i], 0, 0)),
        ],
        out_specs=pl.BlockSpec((tm, half), lambda i, be, s0, st: (i, 0)),
        out_shape=jax.ShapeDtypeStruct((n_rows, half), U32),
        scratch=[
            pltpu.VMEM((2, tm, half), U32),
            pltpu.VMEM((tm, d), BF16),
            pltpu.VMEM((d, 2 * d_ff), BF16),
            pltpu.VMEM((d_ff, d), BF16),
            pltpu.SemaphoreType.DMA((2,)),
        ],
        semantics=("arbitrary",),
        name="moe_experts",
    )(block_e, src0, stok, hp, w_up, w_down)

    return _call(
        functools.partial(_combine_kernel, tt=tt), grid=(n // tt,), num_prefetch=1,
        in_specs=[
            pl.BlockSpec((tt, d), lambda i, p: (i, 0)),
            pl.BlockSpec((tt, LANE), lambda i, p: (i, 0)),
            pl.BlockSpec(memory_space=pl.ANY),
        ],
        out_specs=pl.BlockSpec((tt, d), lambda i, p: (i, 0)),
        out_shape=jax.ShapeDtypeStruct((n, d), F32),
        scratch=[pltpu.VMEM((2, TOP_K * tt, half), U32), pltpu.SemaphoreType.DMA((2,))],
        semantics=("arbitrary",),
        name="moe_combine",
    )(pos, x, gates, y_rows)


def _short_conv_mixer(x, g, w_in, conv_w, w_out, layer, seq):
    d = x.shape[1]
    (y,) = _inproj(x, g, w_in, layer, conv_w[None], jnp.zeros((1, 1, d), F32), hyena=False, seq=seq)
    return _matmul_res(y, w_out, layer, x, "shortconv_out")


def _hyena_mixer(x, g, w_in, conv_w, conv_b, f_w1, f_b1, f_w_hid, f_b_hid, f_freq, f_w_out, skip, w_out,
                 layer, batch, seq):
    d = x.shape[1]
    cw = conv_w.reshape(3, 3, d).transpose(1, 0, 2)
    cb = conv_b.reshape(3, 1, d)
    x0, vv = _inproj(x, g, w_in, layer, cw, cb, hyena=True, seq=seq)
    filt = _hyena_filters(seq, d, f_w1, f_b1, f_w_hid, f_b_hid, f_freq, f_w_out)
    y = _hyena_long_conv(vv, x0, filt, skip[None], batch, seq)
    return _matmul_res(y, w_out, layer, x, "hyena_out")


def kernel(x, norm_mix, norm_ffn, sc_w_in, sc_conv_w, sc_w_out, hy_w_in, hy_conv_w, hy_conv_b, hy_filt_w1, hy_filt_b1, hy_filt_w_hid, hy_filt_b_hid, hy_filt_freq, hy_filt_w_out, hy_skip, hy_w_out, mla_w_down, mla_q_norm, mla_kv_norm, mla_w_uq, mla_w_ukv, mla_q_gain, mla_k_gain, mla_w_o, router_w_group, router_b_group, router_w_expert, router_b_expert, moe_w_up, moe_w_down):
    batch, seq, d = x.shape
    depth = norm_mix.shape[0]
    n_mixers = 3
    xf = x.reshape(batch * seq, d)
    sc_w_in, sc_w_out, hy_w_in, hy_w_out = (w.astype(BF16) for w in (sc_w_in, sc_w_out, hy_w_in, hy_w_out))
    for i in range(depth):
        m = i % n_mixers
        j = i // n_mixers
        g = norm_mix[i][None]
        if m == 0:
            xf = _short_conv_mixer(xf, g, sc_w_in, sc_conv_w[j], sc_w_out, j, seq)
        elif m == 1:
            xf = _hyena_mixer(xf, g, hy_w_in, hy_conv_w[j], hy_conv_b[j], hy_filt_w1[j], hy_filt_b1[j],
                              hy_filt_w_hid[j], hy_filt_b_hid[j], hy_filt_freq[j], hy_filt_w_out[j],
                              hy_skip[j], hy_w_out, j, batch, seq)
        else:
            xf = _mla_mixer(xf, g, mla_w_down[j], mla_q_norm[j], mla_kv_norm[j], mla_w_uq[j], mla_w_ukv[j],
                            mla_q_gain[j], mla_k_gain[j], mla_w_o[j], batch, seq)
        xf = _hier_moe(xf, norm_ffn[i][None], router_w_group[i], router_b_group[i], router_w_expert[i],
                       router_b_expert[i], moe_w_up, moe_w_down, i)
    return xf.reshape(batch, seq, d)
```

```python
import functools
import math

import jax
import jax.numpy as jnp
import numpy as np
from jax import lax
from jax.experimental import pallas as pl
from jax.experimental.pallas import tpu as pltpu

F32 = jnp.float32
BF16 = jnp.bfloat16
I32 = jnp.int32
U32 = jnp.uint32
HIGH_HALF = np.uint32(0xFFFF0000)
EPS = 1e-6

LANE = 128
HALO = 16
VMEM_LIMIT_BYTES = 56 * 1024 * 1024

D_NOPE = 128
D_ROPE = 64
D_V = 128
D_QK = D_NOPE + D_ROPE
D_QK_PAD = 256
D_V_PAD = 256
ROPE_THETA = 10000.0
TOP_K = 2
HY_FAST_DECAY = 0.3
HY_SLOW_DECAY = 1.5
HY_TARGET = 1e-2

HIGHEST = lax.Precision.HIGHEST


def _tile(dim, want):
    t = min(dim, want)
    assert dim % t == 0, (dim, want)
    return t


def _call(kernel, *, grid, in_specs, out_specs, out_shape, semantics, name, scratch=(), num_prefetch=0):
    grid_spec = pltpu.PrefetchScalarGridSpec(
        num_scalar_prefetch=num_prefetch, grid=grid, in_specs=in_specs, out_specs=out_specs,
        scratch_shapes=list(scratch))
    return pl.pallas_call(
        kernel, grid_spec=grid_spec, out_shape=out_shape, name=name,
        compiler_params=pltpu.CompilerParams(dimension_semantics=semantics, vmem_limit_bytes=VMEM_LIMIT_BYTES))


def _dot(a, b, precision=None):
    return jnp.dot(a, b, preferred_element_type=F32, precision=precision)


def _rms(x, g):
    return x * lax.rsqrt(jnp.mean(x * x, axis=-1, keepdims=True) + EPS) * g


def _store_packed_rows(ref, vals):
    rows, d = vals.shape
    half = d // 2
    chunks = half // LANE
    lo_bits = pltpu.bitcast(vals[:, :half].astype(BF16).astype(F32), U32)
    hi_bits = pltpu.bitcast(vals[:, half:].astype(BF16).astype(F32), U32)
    words = (lo_bits >> 16) | (hi_bits & HIGH_HALF)
    for c in range(chunks):
        ref[pl.ds(c, rows, stride=chunks), :] = words[:, c * LANE:(c + 1) * LANE]


def _load_packed_rows(ref, first, rows, chunks):
    out = []
    for c in range(chunks):
        words = ref[pl.ds(first * chunks + c, rows, stride=chunks), :]
        out.append((pltpu.bitcast(words << 16, F32), pltpu.bitcast(words & HIGH_HALF, F32)))
    return out


def _conv3_rows(z, cw, lo, hi, tm):
    rows = z.shape[0]
    r = lax.broadcasted_iota(I32, z.shape, 0)
    z = jnp.where((r < lo) | (r >= hi), 0.0, z)
    zm = pltpu.roll(z, 1, 0)
    zp = pltpu.roll(z, rows - 1, 0)
    y = zm * cw[0:1] + z * cw[1:2] + zp * cw[2:3]
    return y[HALO:HALO + tm]


def _inproj_kernel(x_ref, xp_ref, xn_ref, g_ref, wa_ref, wb_ref, wc_ref, cw_ref, cb_ref, *rest,
                   hyena, seq_tiles, tm):
    hs_ref = rest[-1]
    i = pl.program_id(0)

    @pl.when(pl.program_id(1) == 0)
    def _():
        g = g_ref[...]
        hs_ref[0:HALO, :] = _rms(xp_ref[...], g).astype(BF16)
        hs_ref[HALO:HALO + tm, :] = _rms(x_ref[...], g).astype(BF16)
        hs_ref[HALO + tm:, :] = _rms(xn_ref[...], g).astype(BF16)

    it = i % seq_tiles
    lo = jnp.where(it == 0, HALO, 0)
    hi = jnp.where(it == seq_tiles - 1, HALO + tm, tm + 2 * HALO)
    hs = hs_ref[...]
    za = _dot(hs, wa_ref[...])
    zb = _dot(hs, wb_ref[...])
    zc = _dot(hs, wc_ref[...])
    if hyena:
        x0_ref, vv_ref = rest[0], rest[1]
        x0 = _conv3_rows(za, cw_ref[0], lo, hi, tm) + cb_ref[0]
        x1 = _conv3_rows(zb, cw_ref[1], lo, hi, tm) + cb_ref[1]
        v = _conv3_rows(zc, cw_ref[2], lo, hi, tm) + cb_ref[2]
        x0_ref[...] = x0.astype(BF16)
        vv_ref[...] = (v * x1).astype(BF16)
    else:
        y_ref = rest[0]
        y = za[HALO:HALO + tm] * _conv3_rows(zb * zc, cw_ref[0], lo, hi, tm)
        y_ref[...] = y.astype(BF16)


def _inproj(x, g, w_in, layer, cw, cb, *, hyena, seq):
    n, d = x.shape
    tm = _tile(seq, 512)
    tn = _tile(d, 512)
    nj = d // tn
    hb = tm // HALO
    last_hb = n // HALO - 1
    groups = cw.shape[0]
    kern = functools.partial(_inproj_kernel, hyena=hyena, seq_tiles=seq // tm, tm=tm)
    n_out = 2 if hyena else 1
    out_spec = pl.BlockSpec((tm, tn), lambda i, j: (i, j))
    outs = _call(
        kern, grid=(n // tm, nj),
        in_specs=[
            pl.BlockSpec((tm, d), lambda i, j: (i, 0)),
            pl.BlockSpec((HALO, d), lambda i, j: (jnp.maximum(i * hb - 1, 0), 0)),
            pl.BlockSpec((HALO, d), lambda i, j: (jnp.minimum((i + 1) * hb, last_hb), 0)),
            pl.BlockSpec((1, d), lambda i, j: (0, 0)),
            pl.BlockSpec((None, d, tn), lambda i, j: (layer, 0, j)),
            pl.BlockSpec((None, d, tn), lambda i, j: (layer, 0, nj + j)),
            pl.BlockSpec((None, d, tn), lambda i, j: (layer, 0, 2 * nj + j)),
            pl.BlockSpec((groups, 3, tn), lambda i, j: (0, 0, j)),
            pl.BlockSpec((groups, 1, tn), lambda i, j: (0, 0, j)),
        ],
        out_specs=[out_spec] * n_out,
        out_shape=[jax.ShapeDtypeStruct((n, d), BF16)] * n_out,
        scratch=[pltpu.VMEM((tm + 2 * HALO, d), BF16)],
        semantics=("parallel", "arbitrary"),
        name="hyena_inproj" if hyena else "shortconv_inproj",
    )(x, x, x, g, w_in, w_in, w_in, cw, cb)
    return outs


def _matmul_res_kernel(a_ref, w_ref, r_ref, o_ref):
    o_ref[...] = r_ref[...] + _dot(a_ref[...], w_ref[...])


def _matmul_res(a, w, layer, res, name):
    n, k = a.shape
    dout = w.shape[2]
    tm = _tile(n, 1024)
    tn = _tile(dout, 1024)
    return _call(
        _matmul_res_kernel, grid=(n // tm, dout // tn),
        in_specs=[
            pl.BlockSpec((tm, k), lambda i, j: (i, 0)),
            pl.BlockSpec((None, k, tn), lambda i, j: (layer, 0, j)),
            pl.BlockSpec((tm, tn), lambda i, j: (i, j)),
        ],
        out_specs=pl.BlockSpec((tm, tn), lambda i, j: (i, j)),
        out_shape=jax.ShapeDtypeStruct((n, dout), F32),
        semantics=("parallel", "parallel"),
        name=name,
    )(a, w, res)


def _filter_kernel(z_ref, w1_ref, b1_ref, wh_ref, bh_ref, fr_ref, wo_ref, dl_ref, o_ref, h_ref, *, n_hid, n_half):
    j = pl.program_id(0)

    @pl.when(j == 0)
    def _():
        h = jnp.sin(fr_ref[0:1, :] * (_dot(z_ref[...], w1_ref[...], HIGHEST) + b1_ref[...]))
        for l in range(n_hid):
            h = jnp.sin(fr_ref[l + 1:l + 2, :] * (_dot(h, wh_ref[l], HIGHEST) + bh_ref[l:l + 1, :]))
        h_ref[...] = h

    f = _dot(h_ref[...], wo_ref[...], HIGHEST)
    t = z_ref[:, 0:1]
    f = f * jnp.exp(-t * dl_ref[...])
    row = lax.broadcasted_iota(I32, f.shape, 0)
    f = jnp.where(row < jnp.where(j >= n_half, 1, 0), 0.0, f)
    o_ref[...] = f.astype(BF16)


def _pad_to(a, shape):
    return jnp.pad(a, [(0, s - d) for s, d in zip(shape, a.shape)])


def _hyena_filters(seq, d, f_w1, f_b1, f_w_hid, f_b_hid, f_freq, f_w_out):
    emb, fh = f_w1.shape
    n_hid = f_w_hid.shape[0]
    bands_n = (emb - 1) // 2
    t = jnp.linspace(0.0, 1.0, seq, dtype=F32)[:, None]
    bands = jnp.linspace(1e-4, bands_n - 1, bands_n, dtype=F32)[None]
    w = 2.0 * math.pi * jnp.arange(seq, dtype=F32)[:, None] / seq
    z = jnp.concatenate([t, jnp.cos(bands * w), -jnp.sin(bands * w)], axis=-1)
    max_decay = math.log(HY_TARGET) / HY_FAST_DECAY
    min_decay = math.log(HY_TARGET) / HY_SLOW_DECAY
    deltas = jnp.abs(jnp.linspace(min_decay, max_decay, d, dtype=F32))[None]
    deltas2 = jnp.concatenate([deltas, deltas], axis=-1)
    tn = _tile(d, 512)
    kern = functools.partial(_filter_kernel, n_hid=n_hid, n_half=d // tn)
    full = lambda shape: pl.BlockSpec(shape, lambda j: (0,) * len(shape))
    return _call(
        kern, grid=(2 * d // tn,),
        in_specs=[
            full((seq, LANE)), full((LANE, LANE)), full((1, LANE)), full((n_hid, LANE, LANE)),
            full((n_hid, LANE)), full((n_hid + 1, LANE)),
            pl.BlockSpec((LANE, tn), lambda j: (0, j)),
            pl.BlockSpec((1, tn), lambda j: (0, j)),
        ],
        out_specs=pl.BlockSpec((seq, tn), lambda j: (0, j)),
        out_shape=jax.ShapeDtypeStruct((seq, 2 * d), BF16),
        scratch=[pltpu.VMEM((seq, LANE), F32)],
        semantics=("arbitrary",),
        name="hyena_filter",
    )(_pad_to(z, (seq, LANE)), _pad_to(f_w1, (LANE, LANE)), _pad_to(f_b1[None], (1, LANE)),
      _pad_to(f_w_hid, (n_hid, LANE, LANE)), _pad_to(f_b_hid, (n_hid, LANE)), _pad_to(f_freq, (n_hid + 1, LANE)),
      _pad_to(f_w_out, (LANE, 2 * d)), deltas2)


def _dft_matrices(seq):
    n = 2 * seq
    k = jnp.arange(seq, dtype=I32)

    def cos_sin(freqs):
        ang = ((freqs[:, None] * k[None, :]) % n).astype(F32) * (2.0 * math.pi / n)
        return jnp.cos(ang), jnp.sin(ang)

    rows = _tile(seq, 512)
    blocks = seq // rows
    c0, s0 = cos_sin(jnp.arange(blocks, dtype=I32) * rows)
    c1, s1 = cos_sin(jnp.arange(rows, dtype=I32))
    c0, s0, c1, s1 = lax.optimization_barrier((c0, s0, c1, s1))
    c = (c0[:, None, :] * c1[None] - s0[:, None, :] * s1[None]).reshape(seq, seq)
    s = (s0[:, None, :] * c1[None] + c0[:, None, :] * s1[None]).reshape(seq, seq)
    alt = jnp.where(k % 2 == 0, 1.0, -1.0).astype(F32)
    wc = c
    ws = jnp.where(k[:, None] == 0, alt[None, :], -s)
    wk = jnp.where(k == 0, 1.0, 2.0).astype(F32) / n
    gc = c * wk[None, :]
    gs = jnp.where(k[None, :] == 0, alt[:, None] / n, -s * wk[None, :])
    return wc.astype(BF16), ws.astype(BF16), gc.astype(BF16), gs.astype(BF16)


def _dc_rows(shape, freq_tile):
    row = lax.broadcasted_iota(I32, shape, 0)
    return row < jnp.where(freq_tile == 0, 1, 0)


def _filter_spec_kernel(wc_ref, ws_ref, hf_ref, hb_ref, ha_ref, hbo_ref):
    wc = wc_ref[...]
    ws = ws_ref[...]
    hf = hf_ref[...]
    hb = hb_ref[...]
    ha_ref[...] = _dot(wc, hf) + _dot(wc, hb)
    bf = _dot(ws, hf)
    bb = _dot(ws, hb)
    hbo_ref[...] = jnp.where(_dc_rows(bf.shape, pl.program_id(0)), bf + bb, bf - bb)


def _signal_spec_kernel(wc_ref, ws_ref, v_ref, ha_ref, hb_ref, ya_ref, yb_ref):
    v = v_ref[...]
    a = _dot(wc_ref[...], v)
    b = _dot(ws_ref[...], v)
    ha = ha_ref[...]
    hb = hb_ref[...]
    dc = _dc_rows(a.shape, pl.program_id(0))
    ya_ref[...] = jnp.where(dc, a * ha, a * ha - b * hb).astype(BF16)
    yb_ref[...] = jnp.where(dc, b * hb, a * hb + b * ha).astype(BF16)


def _inverse_kernel(gc_ref, gs_ref, ya_ref, yb_ref, vv_ref, x0_ref, sk_ref, o_ref):
    y = _dot(gc_ref[...], ya_ref[...]) + _dot(gs_ref[...], yb_ref[...])
    y = y + vv_ref[...].astype(F32) * sk_ref[...]
    o_ref[...] = (y * x0_ref[...].astype(F32)).astype(BF16)


def _hyena_long_conv(vv, x0, filt, skip, batch, seq):
    d = vv.shape[1]
    wc, ws, gc, gs = _dft_matrices(seq)
    tk = _tile(seq, 512)
    tn = _tile(d, 512)
    nj = d // tn
    ha, hb = _call(
        _filter_spec_kernel, grid=(seq // tk, nj),
        in_specs=[
            pl.BlockSpec((tk, seq), lambda kt, j: (kt, 0)),
            pl.BlockSpec((tk, seq), lambda kt, j: (kt, 0)),
            pl.BlockSpec((seq, tn), lambda kt, j: (0, j)),
            pl.BlockSpec((seq, tn), lambda kt, j: (0, nj + j)),
        ],
        out_specs=[pl.BlockSpec((tk, tn), lambda kt, j: (kt, j))] * 2,
        out_shape=[jax.ShapeDtypeStruct((seq, d), F32)] * 2,
        semantics=("parallel", "parallel"),
        name="hyena_filter_spectrum",
    )(wc, ws, filt, filt)

    v3 = vv.reshape(batch, seq, d)
    ya, yb = _call(
        _signal_spec_kernel, grid=(seq // tk, batch, nj),
        in_specs=[
            pl.BlockSpec((tk, seq), lambda kt, b, j: (kt, 0)),
            pl.BlockSpec((tk, seq), lambda kt, b, j: (kt, 0)),
            pl.BlockSpec((None, seq, tn), lambda kt, b, j: (b, 0, j)),
            pl.BlockSpec((tk, tn), lambda kt, b, j: (kt, j)),
            pl.BlockSpec((tk, tn), lambda kt, b, j: (kt, j)),
        ],
        out_specs=[pl.BlockSpec((None, tk, tn), lambda kt, b, j: (b, kt, j))] * 2,
        out_shape=[jax.ShapeDtypeStruct((batch, seq, d), BF16)] * 2,
        semantics=("parallel", "parallel", "parallel"),
        name="hyena_signal_spectrum",
    )(wc, ws, v3, ha, hb)

    tt = _tile(seq, 512)
    out = _call(
        _inverse_kernel, grid=(seq // tt, batch, nj),
        in_specs=[
            pl.BlockSpec((tt, seq), lambda ti, b, j: (ti, 0)),
            pl.BlockSpec((tt, seq), lambda ti, b, j: (ti, 0)),
            pl.BlockSpec((None, seq, tn), lambda ti, b, j: (b, 0, j)),
            pl.BlockSpec((None, seq, tn), lambda ti, b, j: (b, 0, j)),
            pl.BlockSpec((None, tt, tn), lambda ti, b, j: (b, ti, j)),
            pl.BlockSpec((None, tt, tn), lambda ti, b, j: (b, ti, j)),
            pl.BlockSpec((1, tn), lambda ti, b, j: (0, j)),
        ],
        out_specs=pl.BlockSpec((None, tt, tn), lambda ti, b, j: (b, ti, j)),
        out_shape=jax.ShapeDtypeStruct((batch, seq, d), BF16),
        semantics=("parallel", "parallel", "parallel"),
        name="hyena_inverse",
    )(gc, gs, ya, yb, v3, x0.reshape(batch, seq, d), skip)
    return out.reshape(batch * seq, d)


def _mla_down_kernel(x_ref, g_ref, w_ref, qn_ref, kvn_ref, cq_ref, ckv_ref, kr_ref, *, q_rank, kv_rank):
    h = _rms(x_ref[...], g_ref[...]).astype(BF16)
    d = _dot(h, w_ref[...])
    cq_ref[...] = _rms(d[:, :q_rank], qn_ref[...]).astype(BF16)
    ckv_ref[...] = _rms(d[:, q_rank:q_rank + kv_rank], kvn_ref[...]).astype(BF16)
    kr_ref[...] = d[:, q_rank + kv_rank:]


def _rope_head(nope, rope, rope_swapped, gains, cos, sin, scale):
    ms = (jnp.sum(nope * nope, axis=-1, keepdims=True) + jnp.sum(rope * rope, axis=-1, keepdims=True)) / D_QK
    r = lax.rsqrt(ms + EPS) * scale
    out_nope = nope * r * gains[:, :LANE]
    out_rope = r * (rope * gains[:, LANE:2 * LANE] * cos + rope_swapped * gains[:, 2 * LANE:] * sin)
    return out_nope, out_rope


def _mla_q_kernel(cq_ref, w_ref, gn_ref, cos_ref, sin_ref, q_ref, *, heads, scale):
    z = _dot(cq_ref[...], w_ref[...])
    cos = cos_ref[...]
    sin = sin_ref[...]
    gains = gn_ref[...]
    for h in range(heads):
        zh = z[:, h * 3 * LANE:(h + 1) * 3 * LANE]
        qn, qr = _rope_head(zh[:, :LANE], zh[:, LANE:2 * LANE], zh[:, 2 * LANE:], gains, cos, sin, scale)
        q_ref[:, h * D_QK_PAD:h * D_QK_PAD + LANE] = qn.astype(BF16)
        q_ref[:, h * D_QK_PAD + LANE:(h + 1) * D_QK_PAD] = qr.astype(BF16)


def _mla_kv_kernel(ckv_ref, w_ref, kr_ref, gn_ref, cos_ref, sin_ref, k_ref, v_ref, *, heads):
    z = _dot(ckv_ref[...], w_ref[...])
    kr = kr_ref[...]
    cos = cos_ref[...]
    sin = sin_ref[...]
    gains = gn_ref[...]
    lane = lax.broadcasted_iota(I32, (z.shape[0], LANE), 1)
    ones_col = jnp.where(lane == 0, 1.0, 0.0).astype(BF16)
    for h in range(heads):
        zh = z[:, h * 2 * LANE:(h + 1) * 2 * LANE]
        kn, krot = _rope_head(zh[:, :LANE], kr[:, :LANE], kr[:, LANE:], gains, cos, sin, 1.0)
        k_ref[:, h * D_QK_PAD:h * D_QK_PAD + LANE] = kn.astype(BF16)
        k_ref[:, h * D_QK_PAD + LANE:(h + 1) * D_QK_PAD] = krot.astype(BF16)
        v_ref[:, h * D_V_PAD:h * D_V_PAD + D_V] = zh[:, LANE:].astype(BF16)
        v_ref[:, h * D_V_PAD + D_V:(h + 1) * D_V_PAD] = ones_col


def _attention_kernel(q_ref, k_ref, v_ref, o_ref, acc_ref, *, tk):
    q = q_ref[...]
    m_old = None
    for c in range(k_ref.shape[0] // tk):
        rows = slice(c * tk, (c + 1) * tk)
        s = lax.dot_general(q, k_ref[rows, :], (((1,), (1,)), ((), ())), preferred_element_type=F32)
        m_new = jnp.max(s, axis=-1, keepdims=True)
        if m_old is not None:
            m_new = jnp.maximum(m_old, m_new)
        pv = _dot(jnp.exp(s - m_new).astype(BF16), v_ref[rows, :])
        if m_old is None:
            acc_ref[...] = pv
        else:
            acc_ref[...] = jnp.exp(m_old - m_new) * acc_ref[...] + pv
        m_old = m_new
    acc = acc_ref[...]
    o_ref[...] = (acc[:, :D_V] * (1.0 / acc[:, D_V:D_V + 1])).astype(BF16)


def _swap_halves(a):
    half = a.shape[-1] // 2
    return jnp.concatenate([a[..., half:], a[..., :half]], axis=-1)


def _lane_pad(a):
    return jnp.pad(a, [(0, 0)] * (a.ndim - 1) + [(0, LANE - a.shape[-1])])


def _rope_gains(gain):
    rope = gain[D_NOPE:]
    return jnp.concatenate([gain[:D_NOPE], _lane_pad(rope), _lane_pad(_swap_halves(rope))])[None]


def _mla_mixer(x, g, w_down, q_norm, kv_norm, w_uq, w_ukv, q_gain, k_gain, w_o, batch, seq):
    n, d = x.shape
    q_rank = q_norm.shape[0]
    kv_rank = kv_norm.shape[0]
    heads = w_o.shape[0] // D_V

    w_kr = w_down[:, q_rank + kv_rank:]
    w_down_p = jnp.concatenate(
        [w_down[:, :q_rank + kv_rank], _lane_pad(w_kr), _lane_pad(_swap_halves(w_kr))], axis=-1).astype(BF16)
    wq = w_uq.reshape(q_rank, heads, D_QK)
    wq_rope = wq[..., D_NOPE:]
    wq_p = jnp.concatenate([wq[..., :D_NOPE], _lane_pad(wq_rope), _lane_pad(_swap_halves(wq_rope))], axis=-1)
    wq_p = wq_p.reshape(q_rank, heads * 3 * LANE).astype(BF16)
    wkv = w_ukv.astype(BF16)

    inv = ROPE_THETA ** (-jnp.arange(0, D_ROPE, 2, dtype=F32) / D_ROPE)
    ang = jnp.arange(seq, dtype=F32)[:, None] * inv[None]
    cos = _lane_pad(jnp.concatenate([jnp.cos(ang), jnp.cos(ang)], axis=-1))
    sin = _lane_pad(jnp.concatenate([-jnp.sin(ang), jnp.sin(ang)], axis=-1))

    tm = _tile(seq, 512)
    seq_tiles = seq // tm
    dw = w_down_p.shape[1]
    cq, ckv, kr = _call(
        functools.partial(_mla_down_kernel, q_rank=q_rank, kv_rank=kv_rank), grid=(n // tm,),
        in_specs=[
            pl.BlockSpec((tm, d), lambda i: (i, 0)),
            pl.BlockSpec((1, d), lambda i: (0, 0)),
            pl.BlockSpec((d, dw), lambda i: (0, 0)),
            pl.BlockSpec((1, q_rank), lambda i: (0, 0)),
            pl.BlockSpec((1, kv_rank), lambda i: (0, 0)),
        ],
        out_specs=[
            pl.BlockSpec((tm, q_rank), lambda i: (i, 0)),
            pl.BlockSpec((tm, kv_rank), lambda i: (i, 0)),
            pl.BlockSpec((tm, 2 * LANE), lambda i: (i, 0)),
        ],
        out_shape=[
            jax.ShapeDtypeStruct((n, q_rank), BF16),
            jax.ShapeDtypeStruct((n, kv_rank), BF16),
            jax.ShapeDtypeStruct((n, 2 * LANE), F32),
        ],
        semantics=("parallel",),
        name="mla_down",
    )(x, g, w_down_p, q_norm[None], kv_norm[None])

    hb = _tile(heads, 4)
    table_spec = pl.BlockSpec((tm, LANE), lambda i, j: (i % seq_tiles, 0))
    q = _call(
        functools.partial(_mla_q_kernel, heads=hb, scale=D_QK ** -0.5), grid=(n // tm, heads // hb),
        in_specs=[
            pl.BlockSpec((tm, q_rank), lambda i, j: (i, 0)),
            pl.BlockSpec((q_rank, hb * 3 * LANE), lambda i, j: (0, j)),
            pl.BlockSpec((1, 3 * LANE), lambda i, j: (0, 0)),
            table_spec, table_spec,
        ],
        out_specs=pl.BlockSpec((tm, hb * D_QK_PAD), lambda i, j: (i, j)),
        out_shape=jax.ShapeDtypeStruct((n, heads * D_QK_PAD), BF16),
        semantics=("parallel", "parallel"),
        name="mla_q",
    )(cq, wq_p, _rope_gains(q_gain), cos, sin)

    k, v = _call(
        functools.partial(_mla_kv_kernel, heads=hb), grid=(n // tm, heads // hb),
        in_specs=[
            pl.BlockSpec((tm, kv_rank), lambda i, j: (i, 0)),
            pl.BlockSpec((kv_rank, hb * 2 * LANE), lambda i, j: (0, j)),
            pl.BlockSpec((tm, 2 * LANE), lambda i, j: (i, 0)),
            pl.BlockSpec((1, 3 * LANE), lambda i, j: (0, 0)),
            table_spec, table_spec,
        ],
        out_specs=[
            pl.BlockSpec((tm, hb * D_QK_PAD), lambda i, j: (i, j)),
            pl.BlockSpec((tm, hb * D_V_PAD), lambda i, j: (i, j)),
        ],
        out_shape=[
            jax.ShapeDtypeStruct((n, heads * D_QK_PAD), BF16),
            jax.ShapeDtypeStruct((n, heads * D_V_PAD), BF16),
        ],
        semantics=("parallel", "parallel"),
        name="mla_kv",
    )(ckv, wkv, kr, _rope_gains(k_gain), cos, sin)

    tq = _tile(seq, 1024)
    qt = seq // tq
    o = _call(
        functools.partial(_attention_kernel, tk=_tile(seq, 512)), grid=(batch, heads, qt),
        in_specs=[
            pl.BlockSpec((tq, D_QK_PAD), lambda b, h, i: (b * qt + i, h)),
            pl.BlockSpec((seq, D_QK_PAD), lambda b, h, i: (b, h)),
            pl.BlockSpec((seq, D_V_PAD), lambda b, h, i: (b, h)),
        ],
        out_specs=pl.BlockSpec((tq, D_V), lambda b, h, i: (b * qt + i, h)),
        out_shape=jax.ShapeDtypeStruct((n, heads * D_V), BF16),
        scratch=[pltpu.VMEM((tq, D_V_PAD), F32)],
        semantics=("parallel", "parallel", "parallel"),
        name="mla_attention",
    )(q, k, v)
    return _matmul_res(o, w_o.astype(BF16)[None], 0, x, "mla_out")


def _router_kernel(x_ref, g_ref, w2_ref, w_hi_ref, b_ref, hp_ref, ids_ref, gates_ref, *, n_groups, per_group):
    h = _rms(x_ref[...], g_ref[...])
    _store_packed_rows(hp_ref, h)
    h_hi = h.astype(BF16)
    h_lo = (h - h_hi.astype(F32)).astype(BF16)
    hw = _dot(h_hi, w2_ref[...])
    logits = hw[:, :LANE] + hw[:, LANE:] + _dot(h_lo, w_hi_ref[...]) + b_ref[...]
    lane = lax.broadcasted_iota(I32, logits.shape, 1)
    neg = -jnp.inf

    def first_argmax(vals, vmax):
        return jnp.min(jnp.where(vals == vmax, lane, LANE), axis=-1, keepdims=True)

    gl = jnp.where(lane < n_groups, logits, neg)
    gmax = jnp.max(gl, axis=-1, keepdims=True)
    g_sel = first_argmax(gl, gmax)
    g_prob = 1.0 / jnp.sum(jnp.exp(gl - gmax), axis=-1, keepdims=True)
    lo = n_groups + g_sel * per_group
    el = jnp.where((lane >= lo) & (lane < lo + per_group), logits, neg)
    t1 = jnp.max(el, axis=-1, keepdims=True)
    i1 = first_argmax(el, t1)
    el2 = jnp.where(lane == i1, neg, el)
    t2 = jnp.max(el2, axis=-1, keepdims=True)
    i2 = first_argmax(el2, t2)
    d = jnp.exp(t2 - t1)
    p1 = 1.0 / (1.0 + d)
    ids_ref[...] = jnp.where(lane == 0, i1 - n_groups, jnp.where(lane == 1, i2 - n_groups, 0))
    gates_ref[...] = jnp.where(lane == 0, g_prob * p1, jnp.where(lane == 1, g_prob * (d * p1), 0.0))


def _start_row_gather(src_hbm, rows_ref, first, dst, sem, count, chunks, unrolled):
    def start(r):
        src_row = pl.multiple_of(rows_ref[first + r] * chunks, chunks)
        dst_row = r * chunks if unrolled else pl.multiple_of(r * chunks, chunks)
        pltpu.make_async_copy(src_hbm.at[pl.ds(src_row, chunks)], dst.at[pl.ds(dst_row, chunks)], sem).start()

    if unrolled:
        for r in range(count):
            start(r)
    else:
        def body(r, carry):
            start(r)
            return carry

        lax.fori_loop(0, count, body, 0)


def _wait_row_gather(dst, sem):
    pltpu.make_async_copy(dst, dst, sem).wait()


def _expert_kernel(block_e_ref, src0_ref, stok_ref, hp_hbm, wup_ref, wdn_ref, y_ref,
                   xbuf, xs_ref, wup_bf, wdn_bf, sem, *, tm, d_ff):
    i = pl.program_id(0)
    last = pl.num_programs(0) - 1
    slot = i % 2

    half = xs_ref.shape[1] // 2
    chunks = half // LANE

    @pl.when(i == 0)
    def _():
        _start_row_gather(hp_hbm, stok_ref, src0_ref[0], xbuf.at[0], sem.at[0], tm, chunks, unrolled=False)

    _wait_row_gather(xbuf.at[slot], sem.at[slot])

    @pl.when((i == 0) | (block_e_ref[i] != block_e_ref[jnp.maximum(i - 1, 0)]))
    def _():
        wup_bf[...] = wup_ref[...].astype(BF16)
        wdn_bf[...] = wdn_ref[...].astype(BF16)

    for c, (lo, hi) in enumerate(_load_packed_rows(xbuf.at[slot], 0, tm, chunks)):
        xs_ref[:, c * LANE:(c + 1) * LANE] = lo.astype(BF16)
        xs_ref[:, half + c * LANE:half + (c + 1) * LANE] = hi.astype(BF16)

    nxt = jnp.minimum(i + 1, last)
    _start_row_gather(hp_hbm, stok_ref, src0_ref[nxt], xbuf.at[1 - slot], sem.at[1 - slot], tm, chunks,
                      unrolled=True)

    hgu = _dot(xs_ref[...], wup_bf[...])
    hg = hgu[:, :d_ff]
    hu = hgu[:, d_ff:]
    act = (hg * jax.nn.sigmoid(hg) * hu).astype(BF16)
    _store_packed_rows(y_ref, _dot(act, wdn_bf[...]))

    @pl.when(i == last)
    def _():
        _wait_row_gather(xbuf.at[1 - slot], sem.at[1 - slot])


def _combine_kernel(pos_ref, x_ref, gates_ref, y_hbm, o_ref, ybuf, sem, *, tt):
    i = pl.program_id(0)
    last = pl.num_programs(0) - 1
    slot = i % 2
    count = TOP_K * tt
    half = x_ref.shape[1] // 2
    chunks = half // LANE

    @pl.when(i == 0)
    def _():
        _start_row_gather(y_hbm, pos_ref, 0, ybuf.at[0], sem.at[0], count, chunks, unrolled=False)

    _wait_row_gather(ybuf.at[slot], sem.at[slot])

    gates = gates_ref[...]
    rows = [_load_packed_rows(ybuf.at[slot], k * tt, tt, chunks) for k in range(TOP_K)]
    for c in range(chunks):
        lo_cols = slice(c * LANE, (c + 1) * LANE)
        hi_cols = slice(half + c * LANE, half + (c + 1) * LANE)
        acc_lo = x_ref[:, lo_cols]
        acc_hi = x_ref[:, hi_cols]
        for k in range(TOP_K):
            lo, hi = rows[k][c]
            acc_lo = acc_lo + gates[:, k:k + 1] * lo
            acc_hi = acc_hi + gates[:, k:k + 1] * hi
        o_ref[:, lo_cols] = acc_lo
        o_ref[:, hi_cols] = acc_hi

    nxt = jnp.minimum(i + 1, last)
    _start_row_gather(y_hbm, pos_ref, nxt * count, ybuf.at[1 - slot], sem.at[1 - slot], count, chunks,
                      unrolled=True)

    @pl.when(i == last)
    def _():
        _wait_row_gather(ybuf.at[1 - slot], sem.at[1 - slot])


def _moe_plan(ids, n_experts, tm, tt):
    n = ids.shape[0]
    a = n * TOP_K
    experts = jnp.arange(n_experts, dtype=I32)
    e2 = ids.reshape(a // LANE, LANE)
    member = e2[None] == experts[:, None, None]
    counts = jnp.sum(member.astype(I32), axis=(1, 2))
    starts = jnp.cumsum(counts) - counts
    padded = (counts + tm - 1) // tm * tm
    pend = jnp.cumsum(padded)
    pstart = pend - padded
    order = jnp.argsort(ids.reshape(a), stable=True).astype(I32)
    rank_sorted = jnp.argsort(order).astype(I32)
    shift = jnp.sum(jnp.where(member, (pstart - starts)[:, None, None], 0), axis=0).reshape(a)
    pos = (rank_sorted + shift).reshape(n // tt, tt, TOP_K).transpose(0, 2, 1).reshape(a)
    n_tiles = (a + n_experts * tm) // tm
    tile_row = jnp.arange(n_tiles, dtype=I32) * tm
    block_e = jnp.minimum(jnp.sum((pend[None, :] <= tile_row[:, None]).astype(I32), axis=1), n_experts - 1)
    pick = block_e[:, None] == experts[None, :]
    tile_shift = jnp.sum(jnp.where(pick, (starts - pstart)[None, :], 0), axis=1)
    src0 = jnp.clip(tile_row + tile_shift, 0, a)
    stok = jnp.concatenate([order // TOP_K, jnp.zeros((tm,), I32)])
    return block_e, src0, stok, pos


def _hier_moe(x, g, w_group, b_group, w_expert, b_expert, w_up, w_down, layer):
    n, d = x.shape
    n_groups = w_group.shape[1]
    n_experts = w_expert.shape[1]
    d_ff = w_down.shape[2]
    chunks = d // (2 * LANE)
    w_r = _pad_to(jnp.concatenate([w_group, w_expert], axis=1), (d, LANE))
    w_hi = w_r.astype(BF16)
    w2 = jnp.concatenate([w_hi, (w_r - w_hi.astype(F32)).astype(BF16)], axis=1)
    b_r = _pad_to(jnp.concatenate([b_group, b_expert])[None], (1, LANE))
    tr = _tile(n, 512)
    hp, ids, gates = _call(
        functools.partial(_router_kernel, n_groups=n_groups, per_group=n_experts // n_groups), grid=(n // tr,),
        in_specs=[
            pl.BlockSpec((tr, d), lambda i: (i, 0)),
            pl.BlockSpec((1, d), lambda i: (0, 0)),
            pl.BlockSpec((d, 2 * LANE), lambda i: (0, 0)),
            pl.BlockSpec((d, LANE), lambda i: (0, 0)),
            pl.BlockSpec((1, LANE), lambda i: (0, 0)),
        ],
        out_specs=[
            pl.BlockSpec((tr * chunks, LANE), lambda i: (i, 0)),
            pl.BlockSpec((tr, LANE), lambda i: (i, 0)),
            pl.BlockSpec((tr, LANE), lambda i: (i, 0)),
        ],
        out_shape=[
            jax.ShapeDtypeStruct((n * chunks, LANE), U32),
            jax.ShapeDtypeStruct((n, LANE), I32),
            jax.ShapeDtypeStruct((n, LANE), F32),
        ],
        semantics=("parallel",),
        name="moe_router",
    )(x, g, w2, w_hi, b_r)

    tm = 256 if n * TOP_K >= 256 * n_experts else 128
    tt = _tile(n, 128)
    block_e, src0, stok, pos = _moe_plan(ids[:, :TOP_K], n_experts, tm, tt)
    n_rows = n * TOP_K + n_experts * tm
    y_rows = _call(
        functools.partial(_expert_kernel, tm=tm, d_ff=d_ff), grid=(n_rows // tm,), num_prefetch=3,
        in_specs=[
            pl.BlockSpec(memory_space=pl.ANY),
            pl.BlockSpec((None, None, d, 2 * d_ff), lambda i, be, s0, st: (layer, be[i], 0, 0)),
            pl.BlockSpec((None, None, d_ff, d), lambda i, be, s0, st: (layer, be[i], 0, 0)),
        ],
        out_specs=pl.BlockSpec((tm * chunks, LANE), lambda i, be, s0, st: (i, 0)),
        out_shape=jax.ShapeDtypeStruct((n_rows * chunks, LANE), U32),
        scratch=[
            pltpu.VMEM((2, tm * chunks, LANE), U32),
            pltpu.VMEM((tm, d), BF16),
            pltpu.VMEM((d, 2 * d_ff), BF16),
            pltpu.VMEM((d_ff, d), BF16),
            pltpu.SemaphoreType.DMA((2,)),
        ],
        semantics=("arbitrary",),
        name="moe_experts",
    )(block_e, src0, stok, hp, w_up, w_down)

    return _call(
        functools.partial(_combine_kernel, tt=tt), grid=(n // tt,), num_prefetch=1,
        in_specs=[
            pl.BlockSpec((tt, d), lambda i, p: (i, 0)),
            pl.BlockSpec((tt, LANE), lambda i, p: (i, 0)),
            pl.BlockSpec(memory_space=pl.ANY),
        ],
        out_specs=pl.BlockSpec((tt, d), lambda i, p: (i, 0)),
        out_shape=jax.ShapeDtypeStruct((n, d), F32),
        scratch=[pltpu.VMEM((2, TOP_K * tt * chunks, LANE), U32), pltpu.SemaphoreType.DMA((2,))],
        semantics=("arbitrary",),
        name="moe_combine",
    )(pos, x, gates, y_rows)


def _short_conv_mixer(x, g, w_in, conv_w, w_out, layer, seq):
    d = x.shape[1]
    (y,) = _inproj(x, g, w_in, layer, conv_w[None], jnp.zeros((1, 1, d), F32), hyena=False, seq=seq)
    return _matmul_res(y, w_out, layer, x, "shortconv_out")


def _hyena_mixer(x, g, w_in, conv_w, conv_b, f_w1, f_b1, f_w_hid, f_b_hid, f_freq, f_w_out, skip, w_out,
                 layer, batch, seq):
    d = x.shape[1]
    cw = conv_w.reshape(3, 3, d).transpose(1, 0, 2)
    cb = conv_b.reshape(3, 1, d)
    x0, vv = _inproj(x, g, w_in, layer, cw, cb, hyena=True, seq=seq)
    filt = _hyena_filters(seq, d, f_w1, f_b1, f_w_hid, f_b_hid, f_freq, f_w_out)
    y = _hyena_long_conv(vv, x0, filt, skip[None], batch, seq)
    return _matmul_res(y, w_out, layer, x, "hyena_out")


def kernel(x, norm_mix, norm_ffn, sc_w_in, sc_conv_w, sc_w_out, hy_w_in, hy_conv_w, hy_conv_b, hy_filt_w1, hy_filt_b1, hy_filt_w_hid, hy_filt_b_hid, hy_filt_freq, hy_filt_w_out, hy_skip, hy_w_out, mla_w_down, mla_q_norm, mla_kv_norm, mla_w_uq, mla_w_ukv, mla_q_gain, mla_k_gain, mla_w_o, router_w_group, router_b_group, router_w_expert, router_b_expert, moe_w_up, moe_w_down):
    batch, seq, d = x.shape
    depth = norm_mix.shape[0]
    n_mixers = 3
    xf = x.reshape(batch * seq, d)
    sc_w_in, sc_w_out, hy_w_in, hy_w_out = (w.astype(BF16) for w in (sc_w_in, sc_w_out, hy_w_in, hy_w_out))
    for i in range(depth):
        m = i % n_mixers
        j = i // n_mixers
        g = norm_mix[i][None]
        if m == 0:
            xf = _short_conv_mixer(xf, g, sc_w_in, sc_conv_w[j], sc_w_out, j, seq)
        elif m == 1:
            xf = _hyena_mixer(xf, g, hy_w_in, hy_conv_w[j], hy_conv_b[j], hy_filt_w1[j], hy_filt_b1[j],
                              hy_filt_w_hid[j], hy_filt_b_hid[j], hy_filt_freq[j], hy_filt_w_out[j],
                              hy_skip[j], hy_w_out, j, batch, seq)
        else:
            xf = _mla_mixer(xf, g, mla_w_down[j], mla_q_norm[j], mla_kv_norm[j], mla_w_uq[j], mla_w_ukv[j],
                            mla_q_gain[j], mla_k_gain[j], mla_w_o[j], batch, seq)
        xf = _hier_moe(xf, norm_ffn[i][None], router_w_group[i], router_b_group[i], router_w_expert[i],
                       router_b_expert[i], moe_w_up, moe_w_down, i)
    return xf.reshape(batch, seq, d)
```
